```python
import jax, jax.numpy as jnp
from jax import lax
import numpy as np

D_MODEL = 2048
BATCH = 2
SEQ = 16384
DEPTH = 4
DEC_BATCH = 8
DEC_SEQ = 16
PAST_LEN = 1024

CHUNK = 64
HEAD_DIM = 128
A_HEADS = 8
A_PREV_CHUNKS = 8
A_REL_MAX = 128
A_REL_SIZE = (CHUNK - 1) + A_REL_MAX + 1
B_HEADS = 8
B_KV_HEADS = 2
B_WINDOW = 128
B_PREV_CHUNKS = B_WINDOW // CHUNK
M_HEADS = 4
M_HEAD_DIM = 256
N_MEM = 256
D_FF = 5632

A_WIDTH = A_HEADS * HEAD_DIM
B_Q_WIDTH = B_HEADS * HEAD_DIM
B_KV_WIDTH = B_KV_HEADS * HEAD_DIM
M_WIDTH = M_HEADS * M_HEAD_DIM
N_BRANCH = 3
IN_SPLITS = (A_WIDTH, A_WIDTH, A_WIDTH, B_Q_WIDTH, B_KV_WIDTH, B_KV_WIDTH, M_WIDTH, N_BRANCH * D_MODEL)
IN_WIDTH = sum(IN_SPLITS)
DEEPNORM_ALPHA = (2 * DEPTH) ** 0.25
DEEPNORM_BETA = (8 * DEPTH) ** -0.25
LN_EPS = 1e-5
NEG_INF = -1e30

kernel_name = 'hybrid_streaming_encoder_step'


def _layernorm(x, g, b):
    xf = x.astype(jnp.float32)
    mu = jnp.mean(xf, axis=-1, keepdims=True)
    var = jnp.mean(jnp.square(xf - mu), axis=-1, keepdims=True)
    y = (xf - mu) * lax.rsqrt(var + LN_EPS) * g.astype(jnp.float32) + b.astype(jnp.float32)
    return y.astype(x.dtype)


def _post_norm(x, f, g, b):
    return _layernorm(DEEPNORM_ALPHA * x + f, g, b)


def _swiglu(x, w_gu, w_down):
    gate, up = jnp.split(x @ w_gu, 2, axis=-1)
    return (jax.nn.silu(gate) * up) @ w_down


def _project(x, w_in):
    b, s, _ = x.shape
    cuts = np.cumsum(IN_SPLITS)[:-1].tolist()
    qa, ka, va, qb, kb, vb, qm, gates = jnp.split(x @ w_in, cuts, axis=-1)
    hd = (b, s, -1, HEAD_DIM)
    return (qa.reshape(hd), ka.reshape(hd), va.reshape(hd),
            qb.reshape(hd), kb.reshape(hd), vb.reshape(hd),
            qm.reshape(b, s, M_HEADS, M_HEAD_DIM), gates.reshape(b, s, N_BRANCH, D_MODEL))


def _rel_dist(offset, n_q, n_k):
    return offset + jnp.arange(n_q)[:, None] - jnp.arange(n_k)[None, :]


def _rel_bias_a(table, rel):
    idx = jnp.clip(rel, -(CHUNK - 1), A_REL_MAX) + (CHUNK - 1)
    return jnp.take(table.astype(jnp.float32), idx, axis=1)


def _alibi_bias(rel):
    slopes = 2.0 ** (-8.0 * (jnp.arange(B_HEADS, dtype=jnp.float32) + 1.0) / B_HEADS)
    return -slopes[:, None, None] * jnp.abs(rel).astype(jnp.float32)


def _to_chunks(x):
    b, s, h, d = x.shape
    return x.reshape(b, s // CHUNK, CHUNK, h, d)


def _gather_band(x, n_prev):
    b, s, g, d = x.shape
    nc = s // CHUNK
    xp = jnp.pad(x.reshape(b, nc, CHUNK, g, d), ((0, 0), (n_prev, 0), (0, 0), (0, 0), (0, 0)))
    return jnp.concatenate([xp[:, j:j + nc] for j in range(n_prev + 1)], axis=2)


def _band_valid(nc, n_prev):
    key_chunk = jnp.arange(nc)[:, None] - n_prev + (jnp.arange((n_prev + 1) * CHUNK) // CHUNK)[None, :]
    return key_chunk >= 0


def _band_attend(q, k, v, bias, valid, sink):
    b, n, nq, h, d = q.shape
    g = k.shape[3]
    r = h // g
    qg = q.reshape(b, n, nq, g, r, d)
    s = jnp.einsum('bnqgrd,bnkgd->bngrqk', qg, k, preferred_element_type=jnp.float32) * (d ** -0.5)
    s = s + bias.reshape(g, r, nq, -1)
    s = jnp.where(valid[None, :, None, None, None, :], s, NEG_INF)
    if sink is None:
        p = jax.nn.softmax(s, axis=-1)
    else:
        sk = sink.astype(jnp.float32).reshape(1, 1, g, r, 1, 1)
        m = jnp.maximum(jnp.max(s, axis=-1, keepdims=True), sk)
        e = jnp.exp(s - m)
        p = e / (jnp.sum(e, axis=-1, keepdims=True) + jnp.exp(sk - m))
    o = jnp.einsum('bngrqk,bnkgd->bnqgrd', p.astype(v.dtype), v)
    return o.reshape(b, n * nq, h * d)


def _mem_kv(mem, w_mem_kv):
    b, n, _ = mem.shape
    mk, mv = jnp.split(mem @ w_mem_kv, 2, axis=-1)
    return mk.reshape(b, n, M_HEADS, M_HEAD_DIM), mv.reshape(b, n, M_HEADS, M_HEAD_DIM)


def _mem_attend(q, mk, mv):
    b, s = q.shape[:2]
    sc = jnp.einsum('bshd,bmhd->bhsm', q, mk.astype(q.dtype), preferred_element_type=jnp.float32) * (M_HEAD_DIM ** -0.5)
    p = jax.nn.softmax(sc, axis=-1).astype(q.dtype)
    return jnp.einsum('bhsm,bmhd->bshd', p, mv.astype(q.dtype)).reshape(b, s, M_WIDTH)


def _merge(gates, oa, ob, om, w_br_a, w_br_b, w_br_m, w_out):
    g = jax.nn.sigmoid(gates.astype(jnp.float32)).astype(oa.dtype)
    h = g[:, :, 0] * (oa @ w_br_a) + g[:, :, 1] * (ob @ w_br_b) + g[:, :, 2] * (om @ w_br_m)
    return h @ w_out


def _mix_prompt(x, mem, w_in, w_br_a, w_br_b, w_br_m, w_out, w_mem_kv, rel_tab, sink):
    b, s, _ = x.shape
    nc = s // CHUNK
    qa, ka, va, qb, kb, vb, qm, gates = _project(x, w_in)
    la = (A_PREV_CHUNKS + 1) * CHUNK
    lb = (B_PREV_CHUNKS + 1) * CHUNK
    oa = _band_attend(_to_chunks(qa), _gather_band(ka, A_PREV_CHUNKS), _gather_band(va, A_PREV_CHUNKS),
                      _rel_bias_a(rel_tab, _rel_dist(A_PREV_CHUNKS * CHUNK, CHUNK, la)),
                      _band_valid(nc, A_PREV_CHUNKS), None)
    ob = _band_attend(_to_chunks(qb), _gather_band(kb, B_PREV_CHUNKS), _gather_band(vb, B_PREV_CHUNKS),
                      _alibi_bias(_rel_dist(B_PREV_CHUNKS * CHUNK, CHUNK, lb)),
                      _band_valid(nc, B_PREV_CHUNKS), sink)
    mk, mv = _mem_kv(mem, w_mem_kv)
    om = _mem_attend(qm, mk, mv)
    y = _merge(gates, oa, ob, om, w_br_a, w_br_b, w_br_m, w_out)
    a_keep = min(A_PREV_CHUNKS * CHUNK, s)
    b_keep = min(B_PREV_CHUNKS * CHUNK, s)
    return y, ka[:, s - a_keep:], va[:, s - a_keep:], kb[:, s - b_keep:], vb[:, s - b_keep:], mk, mv


def _mix_sample(x, ca_k, ca_v, cb_k, cb_v, cm_k, cm_v, w_in, w_br_a, w_br_b, w_br_m, w_out, rel_tab, sink):
    b, t, _ = x.shape
    qa, ka, va, qb, kb, vb, qm, gates = _project(x, w_in)
    wa = ca_k.shape[1]
    wb = cb_k.shape[1]
    ka_all = jnp.concatenate([ca_k.astype(ka.dtype), ka], axis=1)[:, None]
    va_all = jnp.concatenate([ca_v.astype(va.dtype), va], axis=1)[:, None]
    kb_all = jnp.concatenate([cb_k.astype(kb.dtype), kb], axis=1)[:, None]
    vb_all = jnp.concatenate([cb_v.astype(vb.dtype), vb], axis=1)[:, None]
    oa = _band_attend(qa[:, None], ka_all, va_all, _rel_bias_a(rel_tab, _rel_dist(wa, t, wa + t)),
                      jnp.ones((1, wa + t), dtype=bool), None)
    ob = _band_attend(qb[:, None], kb_all, vb_all, _alibi_bias(_rel_dist(wb, t, wb + t)),
                      jnp.ones((1, wb + t), dtype=bool), sink)
    om = _mem_attend(qm, cm_k, cm_v)
    y = _merge(gates, oa, ob, om, w_br_a, w_br_b, w_br_m, w_out)
    return y, ka, va, kb, vb


def setup_inputs(seed: int = 0) -> dict:
    key = jax.random.key(seed)
    ks = jax.random.split(key, 23)
    f32 = jnp.float32

    def nrm(k, shape, scale):
        return jax.random.normal(k, shape, f32) * scale

    a_win = min(A_PREV_CHUNKS * CHUNK, PAST_LEN)
    b_win = min(B_PREV_CHUNKS * CHUNK, PAST_LEN)
    return {
        'x_prompt': nrm(ks[0], (BATCH, SEQ, D_MODEL), 1.0),
        'x_sample': nrm(ks[1], (DEC_BATCH, DEC_SEQ, D_MODEL), 1.0),
        'cache_a_k': nrm(ks[2], (DEPTH, DEC_BATCH, a_win, A_HEADS, HEAD_DIM), 1.0),
        'cache_a_v': nrm(ks[3], (DEPTH, DEC_BATCH, a_win, A_HEADS, HEAD_DIM), 1.0),
        'cache_b_k': nrm(ks[4], (DEPTH, DEC_BATCH, b_win, B_KV_HEADS, HEAD_DIM), 1.0),
        'cache_b_v': nrm(ks[5], (DEPTH, DEC_BATCH, b_win, B_KV_HEADS, HEAD_DIM), 1.0),
        'cache_mem_k': nrm(ks[6], (DEPTH, DEC_BATCH, N_MEM, M_HEADS, M_HEAD_DIM), 1.0),
        'cache_mem_v': nrm(ks[7], (DEPTH, DEC_BATCH, N_MEM, M_HEADS, M_HEAD_DIM), 1.0),
        'mem_prompt': nrm(ks[8], (BATCH, N_MEM, D_MODEL), 1.0),
        'w_in': nrm(ks[9], (DEPTH, D_MODEL, IN_WIDTH), D_MODEL ** -0.5),
        'w_br_a': nrm(ks[10], (DEPTH, A_WIDTH, D_MODEL), A_WIDTH ** -0.5),
        'w_br_b': nrm(ks[11], (DEPTH, B_Q_WIDTH, D_MODEL), B_Q_WIDTH ** -0.5),
        'w_br_m': nrm(ks[12], (DEPTH, M_WIDTH, D_MODEL), M_WIDTH ** -0.5),
        'w_out': nrm(ks[13], (DEPTH, D_MODEL, D_MODEL), DEEPNORM_BETA * D_MODEL ** -0.5),
        'w_mem_kv': nrm(ks[14], (DEPTH, D_MODEL, 2 * M_WIDTH), D_MODEL ** -0.5),
        'rel_bias_a': nrm(ks[15], (DEPTH, A_HEADS, A_REL_SIZE), 0.3),
        'sink_b': nrm(ks[16], (DEPTH, B_HEADS), 0.5),
        'ffn1_gu': nrm(ks[17], (DEPTH, D_MODEL, 2 * D_FF), D_MODEL ** -0.5),
        'ffn1_down': nrm(ks[18], (DEPTH, D_FF, D_MODEL), DEEPNORM_BETA * D_FF ** -0.5),
        'ffn2_gu': nrm(ks[19], (DEPTH, D_MODEL, 2 * D_FF), D_MODEL ** -0.5),
        'ffn2_down': nrm(ks[20], (DEPTH, D_FF, D_MODEL), DEEPNORM_BETA * D_FF ** -0.5),
        'ln_g': 1.0 + nrm(ks[21], (DEPTH, 3, D_MODEL), 0.02),
        'ln_b': nrm(ks[22], (DEPTH, 3, D_MODEL), 0.02),
    }


def reference(x_prompt, x_sample, cache_a_k, cache_a_v, cache_b_k, cache_b_v, cache_mem_k, cache_mem_v,
              mem_prompt, w_in, w_br_a, w_br_b, w_br_m, w_out, w_mem_kv, rel_bias_a, sink_b,
              ffn1_gu, ffn1_down, ffn2_gu, ffn2_down, ln_g, ln_b):
    xp, xs = x_prompt, x_sample
    akp, avp, bkp, bvp, mkp, mvp = [], [], [], [], [], []
    aks, avs, bks, bvs = [], [], [], []
    for l in range(DEPTH):
        xp = _post_norm(xp, 0.5 * _swiglu(xp, ffn1_gu[l], ffn1_down[l]), ln_g[l, 0], ln_b[l, 0])
        mix, ak, av, bk, bv, mk, mv = _mix_prompt(xp, mem_prompt, w_in[l], w_br_a[l], w_br_b[l], w_br_m[l],
                                                  w_out[l], w_mem_kv[l], rel_bias_a[l], sink_b[l])
        xp = _post_norm(xp, mix, ln_g[l, 1], ln_b[l, 1])
        xp = _post_norm(xp, 0.5 * _swiglu(xp, ffn2_gu[l], ffn2_down[l]), ln_g[l, 2], ln_b[l, 2])
        akp.append(ak); avp.append(av); bkp.append(bk); bvp.append(bv); mkp.append(mk); mvp.append(mv)
        xs = _post_norm(xs, 0.5 * _swiglu(xs, ffn1_gu[l], ffn1_down[l]), ln_g[l, 0], ln_b[l, 0])
        mix, ak, av, bk, bv = _mix_sample(xs, cache_a_k[l], cache_a_v[l], cache_b_k[l], cache_b_v[l],
                                          cache_mem_k[l], cache_mem_v[l], w_in[l], w_br_a[l], w_br_b[l],
                                          w_br_m[l], w_out[l], rel_bias_a[l], sink_b[l])
        xs = _post_norm(xs, mix, ln_g[l, 1], ln_b[l, 1])
        xs = _post_norm(xs, 0.5 * _swiglu(xs, ffn2_gu[l], ffn2_down[l]), ln_g[l, 2], ln_b[l, 2])
        aks.append(ak); avs.append(av); bks.append(bk); bvs.append(bv)
    return (xp, xs,
            jnp.stack(akp), jnp.stack(avp), jnp.stack(bkp), jnp.stack(bvp), jnp.stack(mkp), jnp.stack(mvp),
            jnp.stack(aks), jnp.stack(avs), jnp.stack(bks), jnp.stack(bvs))
```

```python
import functools

import numpy as np
import jax
import jax.numpy as jnp
from jax import lax
from jax.experimental import pallas as pl
from jax.experimental.pallas import tpu as pltpu

F32 = jnp.float32
BF16 = jnp.bfloat16

D_MODEL = 2048
DEPTH = 4
CHUNK = 64
HEAD_DIM = 128
A_HEADS = 8
A_PREV_CHUNKS = 8
A_REL_MAX = 128
B_HEADS = 8
B_KV_HEADS = 2
B_PREV_CHUNKS = 2
M_HEADS = 4
M_HEAD_DIM = 256
N_MEM = 256
D_FF = 5632

A_WIDTH = A_HEADS * HEAD_DIM
B_Q_WIDTH = B_HEADS * HEAD_DIM
B_KV_WIDTH = B_KV_HEADS * HEAD_DIM
M_WIDTH = M_HEADS * M_HEAD_DIM
QKV_WIDTH = 3 * A_WIDTH + B_Q_WIDTH + 2 * B_KV_WIDTH + M_WIDTH
COL_QA, COL_KA, COL_VA = 0, A_WIDTH, 2 * A_WIDTH
COL_QB = 3 * A_WIDTH
COL_KB = COL_QB + B_Q_WIDTH
COL_VB = COL_KB + B_KV_WIDTH
COL_QM = COL_VB + B_KV_WIDTH
ATT_WIDTH = A_WIDTH + B_Q_WIDTH + M_WIDTH

DEEPNORM_ALPHA = (2 * DEPTH) ** 0.25
LN_EPS = 1e-5
NEG_INF = -1e30

MIB = 1024 * 1024
QB = 4 * CHUNK
TF = 512
TN_PROJ = 512
TN_MERGE = 1024


def _params(sem, vmem_mib):
    return pltpu.CompilerParams(dimension_semantics=sem, vmem_limit_bytes=vmem_mib * MIB)


def _layernorm(z, g, b):
    mu = jnp.mean(z, axis=-1, keepdims=True)
    zc = z - mu
    var = jnp.mean(zc * zc, axis=-1, keepdims=True)
    return zc * lax.rsqrt(var + LN_EPS) * g + b


def _ffn_body(x_ref, wg_ref, wu_ref, wd_ref, g_ref, b_ref, y_ref, xb_ref, acc_ref, *, nj):
    j = pl.program_id(1)

    @pl.when(j == 0)
    def _():
        xb_ref[...] = x_ref[...].astype(BF16)
        acc_ref[...] = jnp.zeros_like(acc_ref)

    xb = xb_ref[...]
    gate = jnp.dot(xb, wg_ref[...], preferred_element_type=F32)
    up = jnp.dot(xb, wu_ref[...], preferred_element_type=F32)
    h = (gate * jax.nn.sigmoid(gate) * up).astype(BF16)
    acc_ref[...] += jnp.dot(h, wd_ref[...], preferred_element_type=F32)

    @pl.when(j == nj - 1)
    def _():
        z = DEEPNORM_ALPHA * x_ref[...] + 0.5 * acc_ref[...]
        y_ref[...] = _layernorm(z, g_ref[...], b_ref[...])


def _ffn(x, w_gu, w_down, ln_g, ln_b, layer, ln_idx, tm):
    rows = x.shape[0]
    nj = D_FF // TF
    ln_row = layer * 3 + ln_idx
    return pl.pallas_call(
        functools.partial(_ffn_body, nj=nj),
        grid=(rows // tm, nj),
        in_specs=[
            pl.BlockSpec((tm, D_MODEL), lambda i, j: (i, 0)),
            pl.BlockSpec((None, D_MODEL, TF), lambda i, j: (layer, 0, j)),
            pl.BlockSpec((None, D_MODEL, TF), lambda i, j: (layer, 0, j + nj)),
            pl.BlockSpec((None, TF, D_MODEL), lambda i, j: (layer, j, 0)),
            pl.BlockSpec((None, 1, D_MODEL), lambda i, j: (ln_row, 0, 0)),
            pl.BlockSpec((None, 1, D_MODEL), lambda i, j: (ln_row, 0, 0)),
        ],
        out_specs=pl.BlockSpec((tm, D_MODEL), lambda i, j: (i, 0)),
        out_shape=jax.ShapeDtypeStruct((rows, D_MODEL), F32),
        scratch_shapes=[pltpu.VMEM((tm, D_MODEL), BF16), pltpu.VMEM((tm, D_MODEL), F32)],
        compiler_params=_params(("parallel", "arbitrary"), 48),
        name="ffn",
    )(x, w_gu, w_gu, w_down, ln_g, ln_b)


def _proj_body(x_ref, w_ref, o_ref, xb_ref):
    @pl.when(pl.program_id(1) == 0)
    def _():
        xb_ref[...] = x_ref[...].astype(BF16)

    o_ref[...] = jnp.dot(xb_ref[...], w_ref[...], preferred_element_type=F32).astype(o_ref.dtype)


def _proj(x, w, layer, col_block, n_col_blocks, tm, out_dtype):
    rows = x.shape[0]
    return pl.pallas_call(
        _proj_body,
        grid=(rows // tm, n_col_blocks),
        in_specs=[
            pl.BlockSpec((tm, D_MODEL), lambda i, j: (i, 0)),
            pl.BlockSpec((None, D_MODEL, TN_PROJ), lambda i, j: (layer, 0, col_block(j))),
        ],
        out_specs=pl.BlockSpec((tm, TN_PROJ), lambda i, j: (i, j)),
        out_shape=jax.ShapeDtypeStruct((rows, n_col_blocks * TN_PROJ), out_dtype),
        scratch_shapes=[pltpu.VMEM((tm, D_MODEL), BF16)],
        compiler_params=_params(("parallel", "arbitrary"), 40),
        name="proj",
    )(x, w)


def _qk(q, k):
    return lax.dot_general(q, k, (((1,), (1,)), ((), ())), preferred_element_type=F32)


def _softmax_pv(scores, values, sink=None):
    m = scores[0].max(axis=-1, keepdims=True)
    for s in scores[1:]:
        m = jnp.maximum(m, s.max(axis=-1, keepdims=True))
    if sink is not None:
        m = jnp.maximum(m, sink)
    es = [jnp.exp(s - m) for s in scores]
    denom = es[0].sum(axis=-1, keepdims=True)
    for e in es[1:]:
        denom = denom + e.sum(axis=-1, keepdims=True)
    if sink is not None:
        denom = denom + jnp.exp(sink - m)
    o = jnp.dot(es[0].astype(BF16), values[0], preferred_element_type=F32)
    for e, v in zip(es[1:], values[1:]):
        o = o + jnp.dot(e.astype(BF16), v, preferred_element_type=F32)
    return o / denom


def _hs(h, width=HEAD_DIM):
    return slice(h * width, (h + 1) * width)


def _attn_prompt_body(sink_ref, cur_ref, ka2_ref, ka1_ref, va2_ref, va1_ref, kbp_ref, vbp_ref,
                      mk_ref, mv_ref, bias_a_ref, bias_b_ref, o_ref):
    i = pl.program_id(1)
    scale = F32(HEAD_DIM ** -0.5)
    pen2 = jnp.where(i >= 2, F32(0.0), F32(NEG_INF))
    pen1 = jnp.where(i >= 1, F32(0.0), F32(NEG_INF))

    for h in range(A_HEADS):
        q = cur_ref[:, COL_QA + h * HEAD_DIM:COL_QA + (h + 1) * HEAD_DIM]
        ks = [ka2_ref[:, _hs(h)], ka1_ref[:, _hs(h)],
              cur_ref[:, COL_KA + h * HEAD_DIM:COL_KA + (h + 1) * HEAD_DIM]]
        vs = [va2_ref[:, _hs(h)], va1_ref[:, _hs(h)],
              cur_ref[:, COL_VA + h * HEAD_DIM:COL_VA + (h + 1) * HEAD_DIM]]
        s2 = _qk(q, ks[0]) * scale + bias_a_ref[h, :, 0:QB] + pen2
        s1 = _qk(q, ks[1]) * scale + bias_a_ref[h, :, QB:2 * QB] + pen1
        s0 = _qk(q, ks[2]) * scale + bias_a_ref[h, :, 2 * QB:3 * QB]
        o = _softmax_pv([s2, s1, s0], vs)
        o_ref[:, _hs(h)] = o.astype(o_ref.dtype)

    rep = B_HEADS // B_KV_HEADS
    half = QB // 2
    for h in range(B_HEADS):
        g = h // rep
        q = cur_ref[:, COL_QB + h * HEAD_DIM:COL_QB + (h + 1) * HEAD_DIM]
        ks = [kbp_ref[:, _hs(g)], cur_ref[:, COL_KB + g * HEAD_DIM:COL_KB + (g + 1) * HEAD_DIM]]
        vs = [vbp_ref[:, _hs(g)], cur_ref[:, COL_VB + g * HEAD_DIM:COL_VB + (g + 1) * HEAD_DIM]]
        s1 = _qk(q, ks[0]) * scale + bias_b_ref[h, :, 0:half] + pen1
        s0 = _qk(q, ks[1]) * scale + bias_b_ref[h, :, half:half + QB]
        o = _softmax_pv([s1, s0], vs, sink=sink_ref[h])
        o_ref[:, A_WIDTH + h * HEAD_DIM:A_WIDTH + (h + 1) * HEAD_DIM] = o.astype(o_ref.dtype)

    mscale = F32(M_HEAD_DIM ** -0.5)
    for h in range(M_HEADS):
        q = cur_ref[:, COL_QM + h * M_HEAD_DIM:COL_QM + (h + 1) * M_HEAD_DIM]
        s = _qk(q, mk_ref[:, _hs(h, M_HEAD_DIM)]) * mscale
        o = _softmax_pv([s], [mv_ref[:, _hs(h, M_HEAD_DIM)]])
        base = A_WIDTH + B_Q_WIDTH
        o_ref[:, base + h * M_HEAD_DIM:base + (h + 1) * M_HEAD_DIM] = o.astype(o_ref.dtype)


def _attn_prompt(p, mk, mv, bias_a, bias_b, sink, batch, seq):
    nb = seq // QB
    half = QB // 2
    ka_blk, va_blk = COL_KA // A_WIDTH, COL_VA // A_WIDTH
    kb_blk, vb_blk = COL_KB // B_KV_WIDTH, COL_VB // B_KV_WIDTH

    def prev_a(col, back):
        return pl.BlockSpec((QB, A_WIDTH), lambda b, i: (b * nb + jnp.maximum(i - back, 0), col))

    def prev_b(col):
        return pl.BlockSpec((half, B_KV_WIDTH),
                            lambda b, i: (b * 2 * nb + jnp.maximum(2 * i - 1, 0), col))

    return pl.pallas_call(
        _attn_prompt_body,
        grid=(batch, nb),
        in_specs=[
            pl.BlockSpec(memory_space=pltpu.SMEM),
            pl.BlockSpec((QB, QKV_WIDTH), lambda b, i: (b * nb + i, 0)),
            prev_a(ka_blk, 2), prev_a(ka_blk, 1), prev_a(va_blk, 2), prev_a(va_blk, 1),
            prev_b(kb_blk), prev_b(vb_blk),
            pl.BlockSpec((N_MEM, M_WIDTH), lambda b, i: (b, 0)),
            pl.BlockSpec((N_MEM, M_WIDTH), lambda b, i: (b, 0)),
            pl.BlockSpec((A_HEADS, QB, 3 * QB), lambda b, i: (0, 0, 0)),
            pl.BlockSpec((B_HEADS, QB, half + QB), lambda b, i: (0, 0, 0)),
        ],
        out_specs=pl.BlockSpec((QB, ATT_WIDTH), lambda b, i: (b * nb + i, 0)),
        out_shape=jax.ShapeDtypeStruct((batch * seq, ATT_WIDTH), BF16),
        compiler_params=_params(("parallel", "arbitrary"), 48),
        name="attn_prompt",
    )(sink, p, p, p, p, p, p, p, mk, mv, bias_a, bias_b)


def _attn_sample_body(sink_ref, cur_ref, cak_ref, cav_ref, cbk_ref, cbv_ref, cmk_ref, cmv_ref,
                      bias_ac_ref, bias_an_ref, bias_bc_ref, bias_bn_ref, o_ref):
    scale = F32(HEAD_DIM ** -0.5)

    def col(c0, h, width=HEAD_DIM):
        return cur_ref[:, c0 + h * width:c0 + (h + 1) * width].astype(BF16)

    for h in range(A_HEADS):
        q = col(COL_QA, h)
        ks = [cak_ref[:, _hs(h)].astype(BF16), col(COL_KA, h)]
        vs = [cav_ref[:, _hs(h)].astype(BF16), col(COL_VA, h)]
        sc = _qk(q, ks[0]) * scale + bias_ac_ref[h]
        sn = _qk(q, ks[1]) * scale + bias_an_ref[h]
        o_ref[:, _hs(h)] = _softmax_pv([sc, sn], vs).astype(o_ref.dtype)

    rep = B_HEADS // B_KV_HEADS
    for h in range(B_HEADS):
        g = h // rep
        q = col(COL_QB, h)
        ks = [cbk_ref[:, _hs(g)].astype(BF16), col(COL_KB, g)]
        vs = [cbv_ref[:, _hs(g)].astype(BF16), col(COL_VB, g)]
        sc = _qk(q, ks[0]) * scale + bias_bc_ref[h]
        sn = _qk(q, ks[1]) * scale + bias_bn_ref[h]
        o = _softmax_pv([sc, sn], vs, sink=sink_ref[h])
        o_ref[:, A_WIDTH + h * HEAD_DIM:A_WIDTH + (h + 1) * HEAD_DIM] = o.astype(o_ref.dtype)

    mscale = F32(M_HEAD_DIM ** -0.5)
    for h in range(M_HEADS):
        q = col(COL_QM, h, M_HEAD_DIM)
        s = _qk(q, cmk_ref[:, _hs(h, M_HEAD_DIM)].astype(BF16)) * mscale
        o = _softmax_pv([s], [cmv_ref[:, _hs(h, M_HEAD_DIM)].astype(BF16)])
        base = A_WIDTH + B_Q_WIDTH
        o_ref[:, base + h * M_HEAD_DIM:base + (h + 1) * M_HEAD_DIM] = o.astype(o_ref.dtype)


def _attn_sample(p, cak, cav, cbk, cbv, cmk, cmv, biases, sink, layer, streams, t):
    bias_ac, bias_an, bias_bc, bias_bn = biases

    def cache_spec(c):
        return pl.BlockSpec((None, None, c.shape[2], c.shape[3]), lambda s: (layer, s, 0, 0))

    def full_spec(a):
        return pl.BlockSpec(a.shape, lambda s: (0, 0, 0))

    return pl.pallas_call(
        _attn_sample_body,
        grid=(streams,),
        in_specs=[
            pl.BlockSpec(memory_space=pltpu.SMEM),
            pl.BlockSpec((t, QKV_WIDTH), lambda s: (s, 0)),
            cache_spec(cak), cache_spec(cav), cache_spec(cbk), cache_spec(cbv),
            cache_spec(cmk), cache_spec(cmv),
            full_spec(bias_ac), full_spec(bias_an), full_spec(bias_bc), full_spec(bias_bn),
        ],
        out_specs=pl.BlockSpec((t, ATT_WIDTH), lambda s: (s, 0)),
        out_shape=jax.ShapeDtypeStruct((streams * t, ATT_WIDTH), BF16),
        compiler_params=_params(("arbitrary",), 32),
        name="attn_sample",
    )(sink, p, cak, cav, cbk, cbv, cmk, cmv, bias_ac, bias_an, bias_bc, bias_bn)


def _merge_body(x_ref, o_ref, wg_ref, wbr_ref, h_ref, xb_ref, acc_ref, *, n_branch):
    c = pl.program_id(1)
    j = pl.program_id(2)

    @pl.when(jnp.logical_and(c == 0, j == 0))
    def _():
        xb_ref[...] = x_ref[...].astype(BF16)

    gate = jax.nn.sigmoid(jnp.dot(xb_ref[...], wg_ref[...], preferred_element_type=F32))
    term = gate * jnp.dot(o_ref[...], wbr_ref[...], preferred_element_type=F32)

    @pl.when(j == 0)
    def _():
        acc_ref[...] = term

    @pl.when(j > 0)
    def _():
        acc_ref[...] += term

    @pl.when(j == n_branch - 1)
    def _():
        h_ref[...] = acc_ref[...].astype(h_ref.dtype)


def _merge(x, att, w_gate, w_br, layer, tm):
    rows = x.shape[0]
    n_branch = 3
    nc = D_MODEL // TN_MERGE
    return pl.pallas_call(
        functools.partial(_merge_body, n_branch=n_branch),
        grid=(rows // tm, nc, n_branch),
        in_specs=[
            pl.BlockSpec((tm, D_MODEL), lambda i, c, j: (i, 0)),
            pl.BlockSpec((tm, A_WIDTH), lambda i, c, j: (i, j)),
            pl.BlockSpec((None, D_MODEL, TN_MERGE), lambda i, c, j: (layer, 0, j * nc + c)),
            pl.BlockSpec((None, None, A_WIDTH, TN_MERGE), lambda i, c, j: (layer, j, 0, c)),
        ],
        out_specs=pl.BlockSpec((tm, TN_MERGE), lambda i, c, j: (i, c)),
        out_shape=jax.ShapeDtypeStruct((rows, D_MODEL), BF16),
        scratch_shapes=[pltpu.VMEM((tm, D_MODEL), BF16), pltpu.VMEM((tm, TN_MERGE), F32)],
        compiler_params=_params(("parallel", "arbitrary", "arbitrary"), 48),
        name="merge",
    )(x, att, w_gate, w_br)


def _outproj_body(x_ref, h_ref, w_ref, g_ref, b_ref, y_ref):
    mix = jnp.dot(h_ref[...], w_ref[...], preferred_element_type=F32)
    z = DEEPNORM_ALPHA * x_ref[...] + mix
    y_ref[...] = _layernorm(z, g_ref[...], b_ref[...])


def _outproj(x, h, w_out, ln_g, ln_b, layer, tm):
    rows = x.shape[0]
    ln_row = layer * 3 + 1
    return pl.pallas_call(
        _outproj_body,
        grid=(rows // tm,),
        in_specs=[
            pl.BlockSpec((tm, D_MODEL), lambda i: (i, 0)),
            pl.BlockSpec((tm, D_MODEL), lambda i: (i, 0)),
            pl.BlockSpec((None, D_MODEL, D_MODEL), lambda i: (layer, 0, 0)),
            pl.BlockSpec((None, 1, D_MODEL), lambda i: (ln_row, 0, 0)),
            pl.BlockSpec((None, 1, D_MODEL), lambda i: (ln_row, 0, 0)),
        ],
        out_specs=pl.BlockSpec((tm, D_MODEL), lambda i: (i, 0)),
        out_shape=jax.ShapeDtypeStruct((rows, D_MODEL), F32),
        compiler_params=_params(("parallel",), 48),
        name="outproj",
    )(x, h, w_out, ln_g, ln_b)


def _rel_table_bias(table, rel):
    idx = np.clip(rel, -(CHUNK - 1), A_REL_MAX) + (CHUNK - 1)
    return jnp.take(table.astype(F32), jnp.asarray(idx), axis=1)


def _alibi(rel):
    slopes = 2.0 ** (-8.0 * (jnp.arange(B_HEADS, dtype=F32) + 1.0) / B_HEADS)
    return -slopes[:, None, None] * jnp.asarray(np.abs(rel), dtype=F32)


def _band(n_keys, n_prev):
    cq = np.arange(QB)[:, None] // CHUNK
    ck = np.arange(n_keys)[None, :] // CHUNK
    return (ck >= cq) & (ck <= cq + n_prev)


def _prompt_biases(table):
    rel_a = A_PREV_CHUNKS * CHUNK + np.arange(QB)[:, None] - np.arange(3 * QB)[None, :]
    bias_a = jnp.where(jnp.asarray(_band(3 * QB, A_PREV_CHUNKS))[None], _rel_table_bias(table, rel_a), NEG_INF)
    nkb = QB // 2 + QB
    rel_b = B_PREV_CHUNKS * CHUNK + np.arange(QB)[:, None] - np.arange(nkb)[None, :]
    bias_b = jnp.where(jnp.asarray(_band(nkb, B_PREV_CHUNKS))[None], _alibi(rel_b), NEG_INF)
    return bias_a, bias_b


def _sample_biases(table, wa, wb, t):
    rel_a = wa + np.arange(t)[:, None] - np.arange(wa + t)[None, :]
    ba = _rel_table_bias(table, rel_a)
    rel_b = wb + np.arange(t)[:, None] - np.arange(wb + t)[None, :]
    bb = _alibi(rel_b)
    return ba[:, :, :wa], ba[:, :, wa:], bb[:, :, :wb], bb[:, :, wb:]


def kernel(x_prompt, x_sample, cache_a_k, cache_a_v, cache_b_k, cache_b_v, cache_mem_k, cache_mem_v,
           mem_prompt, w_in, w_br_a, w_br_b, w_br_m, w_out, w_mem_kv, rel_bias_a, sink_b,
           ffn1_gu, ffn1_down, ffn2_gu, ffn2_down, ln_g, ln_b):
    batch, seq, _ = x_prompt.shape
    streams, t, _ = x_sample.shape
    a_keep = min(A_PREV_CHUNKS * CHUNK, seq)
    b_keep = min(B_PREV_CHUNKS * CHUNK, seq)
    wa, wb = cache_a_k.shape[2], cache_b_k.shape[2]

    w_qkv = w_in[:, :, :QKV_WIDTH].astype(BF16)
    w_gate = w_in[:, :, QKV_WIDTH:].astype(BF16)
    w_br = jnp.stack([w_br_a, w_br_b, w_br_m], axis=1).astype(BF16)
    w_out_b = w_out.astype(BF16)
    w_mem_b = w_mem_kv.astype(BF16)
    f1gu, f1d = ffn1_gu.astype(BF16), ffn1_down.astype(BF16)
    f2gu, f2d = ffn2_gu.astype(BF16), ffn2_down.astype(BF16)
    lng = ln_g.reshape(DEPTH * 3, 1, D_MODEL)
    lnb = ln_b.reshape(DEPTH * 3, 1, D_MODEL)

    cak = cache_a_k.reshape(DEPTH, streams, wa, A_WIDTH)
    cav = cache_a_v.reshape(DEPTH, streams, wa, A_WIDTH)
    cbk = cache_b_k.reshape(DEPTH, streams, wb, B_KV_WIDTH)
    cbv = cache_b_v.reshape(DEPTH, streams, wb, B_KV_WIDTH)
    cmk = cache_mem_k.reshape(DEPTH, streams, N_MEM, M_WIDTH)
    cmv = cache_mem_v.reshape(DEPTH, streams, N_MEM, M_WIDTH)

    xp = x_prompt.reshape(batch * seq, D_MODEL)
    xs = x_sample.reshape(streams * t, D_MODEL)
    mem = mem_prompt.reshape(batch * N_MEM, D_MODEL)

    tm_p, tm_s = 512, streams * t
    n_qkv_blocks = QKV_WIDTH // TN_PROJ
    n_mem_blocks = 2 * M_WIDTH // TN_PROJ
    kv_blocks = (COL_KA // TN_PROJ, COL_KB // TN_PROJ)
    n_kv_a = 2 * A_WIDTH // TN_PROJ

    akp, avp, bkp, bvp, mkp, mvp = [], [], [], [], [], []
    aks, avs, bks, bvs = [], [], [], []
    for l in range(DEPTH):
        bias_a, bias_b = _prompt_biases(rel_bias_a[l])
        s_biases = _sample_biases(rel_bias_a[l], wa, wb, t)
        sink = sink_b[l].astype(F32)

        xp = _ffn(xp, f1gu, f1d, lng, lnb, l, 0, tm_p)
        p = _proj(xp, w_qkv, l, lambda j: j, n_qkv_blocks, 1024, BF16)
        tail = xp.reshape(batch, seq, D_MODEL)[:, seq - a_keep:].reshape(batch * a_keep, D_MODEL)
        kv = _proj(tail, w_qkv, l, lambda j: jnp.where(j < n_kv_a, j + kv_blocks[0], kv_blocks[1]),
                   n_kv_a + 1, 512, F32).reshape(batch, a_keep, -1)
        mkv = _proj(mem, w_mem_b, l, lambda j: j, n_mem_blocks, 512, F32)
        mk, mv = mkv[:, :M_WIDTH], mkv[:, M_WIDTH:]
        att = _attn_prompt(p, mk.astype(BF16), mv.astype(BF16), bias_a, bias_b, sink, batch, seq)
        h = _merge(xp, att, w_gate, w_br, l, tm_p)
        xp = _outproj(xp, h, w_out_b, lng, lnb, l, tm_p)
        xp = _ffn(xp, f2gu, f2d, lng, lnb, l, 2, tm_p)

        akp.append(kv[:, :, :A_WIDTH].reshape(batch, a_keep, A_HEADS, HEAD_DIM))
        avp.append(kv[:, :, A_WIDTH:2 * A_WIDTH].reshape(batch, a_keep, A_HEADS, HEAD_DIM))
        kvb = kv[:, a_keep - b_keep:, 2 * A_WIDTH:]
        bkp.append(kvb[:, :, :B_KV_WIDTH].reshape(batch, b_keep, B_KV_HEADS, HEAD_DIM))
        bvp.append(kvb[:, :, B_KV_WIDTH:].reshape(batch, b_keep, B_KV_HEADS, HEAD_DIM))
        mkp.append(mk.reshape(batch, N_MEM, M_HEADS, M_HEAD_DIM))
        mvp.append(mv.reshape(batch, N_MEM, M_HEADS, M_HEAD_DIM))

        xs = _ffn(xs, f1gu, f1d, lng, lnb, l, 0, tm_s)
        ps = _proj(xs, w_qkv, l, lambda j: j, n_qkv_blocks, tm_s, F32)
        att_s = _attn_sample(ps, cak, cav, cbk, cbv, cmk, cmv, s_biases, sink, l, streams, t)
        hs = _merge(xs, att_s, w_gate, w_br, l, tm_s)
        xs = _outproj(xs, hs, w_out_b, lng, lnb, l, tm_s)
        xs = _ffn(xs, f2gu, f2d, lng, lnb, l, 2, tm_s)

        ps3 = ps.reshape(streams, t, QKV_WIDTH)
        aks.append(ps3[:, :, COL_KA:COL_KA + A_WIDTH].reshape(streams, t, A_HEADS, HEAD_DIM))
        avs.append(ps3[:, :, COL_VA:COL_VA + A_WIDTH].reshape(streams, t, A_HEADS, HEAD_DIM))
        bks.append(ps3[:, :, COL_KB:COL_KB + B_KV_WIDTH].reshape(streams, t, B_KV_HEADS, HEAD_DIM))
        bvs.append(ps3[:, :, COL_VB:COL_VB + B_KV_WIDTH].reshape(streams, t, B_KV_HEADS, HEAD_DIM))

    return (xp.reshape(batch, seq, D_MODEL), xs.reshape(streams, t, D_MODEL),
            jnp.stack(akp), jnp.stack(avp), jnp.stack(bkp), jnp.stack(bvp), jnp.stack(mkp), jnp.stack(mvp),
            jnp.stack(aks), jnp.stack(avs), jnp.stack(bks), jnp.stack(bvs))
```

```python
import functools

import numpy as np
import jax
import jax.numpy as jnp
from jax import lax
from jax.experimental import pallas as pl
from jax.experimental.pallas import tpu as pltpu

F32 = jnp.float32
BF16 = jnp.bfloat16

D_MODEL = 2048
DEPTH = 4
CHUNK = 64
HEAD_DIM = 128
A_HEADS = 8
A_PREV_CHUNKS = 8
A_REL_MAX = 128
B_HEADS = 8
B_KV_HEADS = 2
B_PREV_CHUNKS = 2
M_HEADS = 4
M_HEAD_DIM = 256
N_MEM = 256
D_FF = 5632

A_WIDTH = A_HEADS * HEAD_DIM
B_Q_WIDTH = B_HEADS * HEAD_DIM
B_KV_WIDTH = B_KV_HEADS * HEAD_DIM
M_WIDTH = M_HEADS * M_HEAD_DIM
QKV_WIDTH = 3 * A_WIDTH + B_Q_WIDTH + 2 * B_KV_WIDTH + M_WIDTH
COL_QA, COL_KA, COL_VA = 0, A_WIDTH, 2 * A_WIDTH
COL_QB = 3 * A_WIDTH
COL_KB = COL_QB + B_Q_WIDTH
COL_VB = COL_KB + B_KV_WIDTH
COL_QM = COL_VB + B_KV_WIDTH
ATT_WIDTH = A_WIDTH + B_Q_WIDTH + M_WIDTH

DEEPNORM_ALPHA = (2 * DEPTH) ** 0.25
LN_EPS = 1e-5
NEG_INF = -1e30

MIB = 1024 * 1024
QB = 4 * CHUNK
TF = 512
TN_QKV = QKV_WIDTH // 2
TN_SMALL = 512


def _params(sem, vmem_mib):
    return pltpu.CompilerParams(dimension_semantics=sem, vmem_limit_bytes=vmem_mib * MIB)


def _layernorm(z, g, b):
    mu = jnp.mean(z, axis=-1, keepdims=True)
    zc = z - mu
    var = jnp.mean(zc * zc, axis=-1, keepdims=True)
    return zc * lax.rsqrt(var + LN_EPS) * g + b


def _row_chunks(tm, n_chunks):
    rc = tm // n_chunks
    return [slice(r * rc, (r + 1) * rc) for r in range(n_chunks)]


def _store_norm(z, g_ref, b_ref, y_ref, yb_ref, rows):
    y = _layernorm(z, g_ref[...], b_ref[...])
    y_ref[rows, :] = y
    yb_ref[rows, :] = y.astype(BF16)


def _ffn_body(x_ref, xb_ref, wg_ref, wu_ref, wd_ref, g_ref, b_ref, y_ref, yb_ref, acc_ref, *, nj, n_chunks):
    j = pl.program_id(1)

    def step(first, last):
        for rows in _row_chunks(x_ref.shape[0], n_chunks):
            xb = xb_ref[rows, :]
            gate = jnp.dot(xb, wg_ref[...], preferred_element_type=F32)
            up = jnp.dot(xb, wu_ref[...], preferred_element_type=F32)
            h = (gate * jax.nn.sigmoid(gate) * up).astype(BF16)
            d = jnp.dot(h, wd_ref[...], preferred_element_type=F32)
            if last:
                z = DEEPNORM_ALPHA * x_ref[rows, :] + 0.5 * (acc_ref[rows, :] + d)
                _store_norm(z, g_ref, b_ref, y_ref, yb_ref, rows)
            elif first:
                acc_ref[rows, :] = d
            else:
                acc_ref[rows, :] += d

    pl.when(j == 0)(lambda: step(True, False))
    pl.when(jnp.logical_and(j > 0, j < nj - 1))(lambda: step(False, False))
    pl.when(j == nj - 1)(lambda: step(False, True))


def _ffn(x, xb, w_gu, w_down, ln_g, ln_b, layer, ln_idx, tm, n_chunks):
    rows = x.shape[0]
    nj = D_FF // TF
    ln_row = layer * 3 + ln_idx
    row_spec = pl.BlockSpec((tm, D_MODEL), lambda i, j: (i, 0))
    return pl.pallas_call(
        functools.partial(_ffn_body, nj=nj, n_chunks=n_chunks),
        grid=(rows // tm, nj),
        in_specs=[
            row_spec, row_spec,
            pl.BlockSpec((None, D_MODEL, TF), lambda i, j: (layer, 0, j)),
            pl.BlockSpec((None, D_MODEL, TF), lambda i, j: (layer, 0, j + nj)),
            pl.BlockSpec((None, TF, D_MODEL), lambda i, j: (layer, j, 0)),
            pl.BlockSpec((None, 1, D_MODEL), lambda i, j: (ln_row, 0, 0)),
            pl.BlockSpec((None, 1, D_MODEL), lambda i, j: (ln_row, 0, 0)),
        ],
        out_specs=[row_spec, row_spec],
        out_shape=[jax.ShapeDtypeStruct((rows, D_MODEL), F32), jax.ShapeDtypeStruct((rows, D_MODEL), BF16)],
        scratch_shapes=[pltpu.VMEM((tm, D_MODEL), F32)],
        compiler_params=_params(("parallel", "arbitrary"), 52),
        name="ffn",
    )(x, xb, w_gu, w_gu, w_down, ln_g, ln_b)


def _proj_body(xb_ref, w_ref, o_ref, *, n_chunks):
    for rows in _row_chunks(xb_ref.shape[0], n_chunks):
        o_ref[rows, :] = jnp.dot(xb_ref[rows, :], w_ref[...], preferred_element_type=F32).astype(o_ref.dtype)


def _proj(xb, w, layer, tn, col_block, n_col_blocks, tm, n_chunks, out_dtype):
    rows = xb.shape[0]
    return pl.pallas_call(
        functools.partial(_proj_body, n_chunks=n_chunks),
        grid=(rows // tm, n_col_blocks),
        in_specs=[
            pl.BlockSpec((tm, D_MODEL), lambda i, j: (i, 0)),
            pl.BlockSpec((None, D_MODEL, tn), lambda i, j: (layer, 0, col_block(j))),
        ],
        out_specs=pl.BlockSpec((tm, tn), lambda i, j: (i, j)),
        out_shape=jax.ShapeDtypeStruct((rows, n_col_blocks * tn), out_dtype),
        compiler_params=_params(("parallel", "arbitrary"), 52),
        name="proj",
    )(xb, w)


def _qk(q, k):
    return lax.dot_general(q, k, (((1,), (1,)), ((), ())), preferred_element_type=F32)


def _softmax_pv(scores, values, sink=None):
    m = scores[0].max(axis=-1, keepdims=True)
    for s in scores[1:]:
        m = jnp.maximum(m, s.max(axis=-1, keepdims=True))
    if sink is not None:
        m = jnp.maximum(m, sink)
    es = [jnp.exp(s - m) for s in scores]
    denom = es[0].sum(axis=-1, keepdims=True)
    for e in es[1:]:
        denom = denom + e.sum(axis=-1, keepdims=True)
    if sink is not None:
        denom = denom + jnp.exp(sink - m)
    o = jnp.dot(es[0].astype(BF16), values[0], preferred_element_type=F32)
    for e, v in zip(es[1:], values[1:]):
        o = o + jnp.dot(e.astype(BF16), v, preferred_element_type=F32)
    return o / denom


def _hs(h, width=HEAD_DIM):
    return slice(h * width, (h + 1) * width)


def _attn_prompt_body(sink_ref, cur_ref, ka2_ref, ka1_ref, va2_ref, va1_ref, kbp_ref, vbp_ref,
                      mk_ref, mv_ref, bias_a_ref, bias_b_ref, o_ref):
    i = pl.program_id(1)
    scale = F32(HEAD_DIM ** -0.5)
    rep = B_HEADS // B_KV_HEADS
    half = QB // 2

    def cur(c0, h, width=HEAD_DIM):
        return cur_ref[:, c0 + h * width:c0 + (h + 1) * width]

    def branches_ab(pen2, pen1):
        for h in range(A_HEADS):
            q = cur(COL_QA, h)
            s2 = _qk(q, ka2_ref[:, _hs(h)]) * scale + bias_a_ref[h, :, 0:QB]
            s1 = _qk(q, ka1_ref[:, _hs(h)]) * scale + bias_a_ref[h, :, QB:2 * QB]
            s0 = _qk(q, cur(COL_KA, h)) * scale + bias_a_ref[h, :, 2 * QB:3 * QB]
            if pen2 is not None:
                s2, s1 = s2 + pen2, s1 + pen1
            vs = [va2_ref[:, _hs(h)], va1_ref[:, _hs(h)], cur(COL_VA, h)]
            o_ref[:, _hs(h)] = _softmax_pv([s2, s1, s0], vs).astype(o_ref.dtype)

        for h in range(B_HEADS):
            g = h // rep
            q = cur(COL_QB, h)
            s1 = _qk(q, kbp_ref[:, _hs(g)]) * scale + bias_b_ref[h, :, 0:half]
            s0 = _qk(q, cur(COL_KB, g)) * scale + bias_b_ref[h, :, half:half + QB]
            if pen1 is not None:
                s1 = s1 + pen1
            o = _softmax_pv([s1, s0], [vbp_ref[:, _hs(g)], cur(COL_VB, g)], sink=sink_ref[h])
            o_ref[:, A_WIDTH + h * HEAD_DIM:A_WIDTH + (h + 1) * HEAD_DIM] = o.astype(o_ref.dtype)

    @pl.when(i >= A_PREV_CHUNKS * CHUNK // QB)
    def _():
        branches_ab(None, None)

    @pl.when(i < A_PREV_CHUNKS * CHUNK // QB)
    def _():
        branches_ab(jnp.where(i >= 2, F32(0.0), F32(NEG_INF)), jnp.where(i >= 1, F32(0.0), F32(NEG_INF)))

    mscale = F32(M_HEAD_DIM ** -0.5)
    for h in range(M_HEADS):
        s = _qk(cur(COL_QM, h, M_HEAD_DIM), mk_ref[:, _hs(h, M_HEAD_DIM)]) * mscale
        o = _softmax_pv([s], [mv_ref[:, _hs(h, M_HEAD_DIM)]])
        base = A_WIDTH + B_Q_WIDTH
        o_ref[:, base + h * M_HEAD_DIM:base + (h + 1) * M_HEAD_DIM] = o.astype(o_ref.dtype)


def _attn_prompt(p, mk, mv, bias_a, bias_b, sink, batch, seq):
    nb = seq // QB
    half = QB // 2
    ka_blk, va_blk = COL_KA // A_WIDTH, COL_VA // A_WIDTH
    kb_blk, vb_blk = COL_KB // B_KV_WIDTH, COL_VB // B_KV_WIDTH

    def prev_a(col, back):
        return pl.BlockSpec((QB, A_WIDTH), lambda b, i: (b * nb + jnp.maximum(i - back, 0), col))

    def prev_b(col):
        return pl.BlockSpec((half, B_KV_WIDTH),
                            lambda b, i: (b * 2 * nb + jnp.maximum(2 * i - 1, 0), col))

    return pl.pallas_call(
        _attn_prompt_body,
        grid=(batch, nb),
        in_specs=[
            pl.BlockSpec(memory_space=pltpu.SMEM),
            pl.BlockSpec((QB, QKV_WIDTH), lambda b, i: (b * nb + i, 0)),
            prev_a(ka_blk, 2), prev_a(ka_blk, 1), prev_a(va_blk, 2), prev_a(va_blk, 1),
            prev_b(kb_blk), prev_b(vb_blk),
            pl.BlockSpec((N_MEM, M_WIDTH), lambda b, i: (b, 0)),
            pl.BlockSpec((N_MEM, M_WIDTH), lambda b, i: (b, 0)),
            pl.BlockSpec((A_HEADS, QB, 3 * QB), lambda b, i: (0, 0, 0)),
            pl.BlockSpec((B_HEADS, QB, half + QB), lambda b, i: (0, 0, 0)),
        ],
        out_specs=pl.BlockSpec((QB, ATT_WIDTH), lambda b, i: (b * nb + i, 0)),
        out_shape=jax.ShapeDtypeStruct((batch * seq, ATT_WIDTH), BF16),
        compiler_params=_params(("parallel", "arbitrary"), 48),
        name="attn_prompt",
    )(sink, p, p, p, p, p, p, p, mk, mv, bias_a, bias_b)


def _attn_sample_body(sink_ref, cur_ref, cak_ref, cav_ref, cbk_ref, cbv_ref, cmk_ref, cmv_ref,
                      bias_ac_ref, bias_an_ref, bias_bc_ref, bias_bn_ref, o_ref):
    scale = F32(HEAD_DIM ** -0.5)

    def col(c0, h, width=HEAD_DIM):
        return cur_ref[:, c0 + h * width:c0 + (h + 1) * width].astype(BF16)

    for h in range(A_HEADS):
        q = col(COL_QA, h)
        ks = [cak_ref[:, _hs(h)].astype(BF16), col(COL_KA, h)]
        vs = [cav_ref[:, _hs(h)].astype(BF16), col(COL_VA, h)]
        sc = _qk(q, ks[0]) * scale + bias_ac_ref[h]
        sn = _qk(q, ks[1]) * scale + bias_an_ref[h]
        o_ref[:, _hs(h)] = _softmax_pv([sc, sn], vs).astype(o_ref.dtype)

    rep = B_HEADS // B_KV_HEADS
    for h in range(B_HEADS):
        g = h // rep
        q = col(COL_QB, h)
        ks = [cbk_ref[:, _hs(g)].astype(BF16), col(COL_KB, g)]
        vs = [cbv_ref[:, _hs(g)].astype(BF16), col(COL_VB, g)]
        sc = _qk(q, ks[0]) * scale + bias_bc_ref[h]
        sn = _qk(q, ks[1]) * scale + bias_bn_ref[h]
        o = _softmax_pv([sc, sn], vs, sink=sink_ref[h])
        o_ref[:, A_WIDTH + h * HEAD_DIM:A_WIDTH + (h + 1) * HEAD_DIM] = o.astype(o_ref.dtype)

    mscale = F32(M_HEAD_DIM ** -0.5)
    for h in range(M_HEADS):
        q = col(COL_QM, h, M_HEAD_DIM)
        s = _qk(q, cmk_ref[:, _hs(h, M_HEAD_DIM)].astype(BF16)) * mscale
        o = _softmax_pv([s], [cmv_ref[:, _hs(h, M_HEAD_DIM)].astype(BF16)])
        base = A_WIDTH + B_Q_WIDTH
        o_ref[:, base + h * M_HEAD_DIM:base + (h + 1) * M_HEAD_DIM] = o.astype(o_ref.dtype)


def _attn_sample(p, cak, cav, cbk, cbv, cmk, cmv, biases, sink, layer, streams, t):
    bias_ac, bias_an, bias_bc, bias_bn = biases

    def cache_spec(c):
        return pl.BlockSpec((None, None, c.shape[2], c.shape[3]), lambda s: (layer, s, 0, 0))

    def full_spec(a):
        return pl.BlockSpec(a.shape, lambda s: (0, 0, 0))

    return pl.pallas_call(
        _attn_sample_body,
        grid=(streams,),
        in_specs=[
            pl.BlockSpec(memory_space=pltpu.SMEM),
            pl.BlockSpec((t, QKV_WIDTH), lambda s: (s, 0)),
            cache_spec(cak), cache_spec(cav), cache_spec(cbk), cache_spec(cbv),
            cache_spec(cmk), cache_spec(cmv),
            full_spec(bias_ac), full_spec(bias_an), full_spec(bias_bc), full_spec(bias_bn),
        ],
        out_specs=pl.BlockSpec((t, ATT_WIDTH), lambda s: (s, 0)),
        out_shape=jax.ShapeDtypeStruct((streams * t, ATT_WIDTH), BF16),
        compiler_params=_params(("arbitrary",), 32),
        name="attn_sample",
    )(sink, p, cak, cav, cbk, cbv, cmk, cmv, bias_ac, bias_an, bias_bc, bias_bn)


def _merge_body(xb_ref, o_ref, wg_ref, wbr_ref, h_ref, acc_ref, *, n_branch, n_chunks):
    j = pl.program_id(1)

    def step(first, last):
        for rows in _row_chunks(xb_ref.shape[0], n_chunks):
            gate = jax.nn.sigmoid(jnp.dot(xb_ref[rows, :], wg_ref[...], preferred_element_type=F32))
            term = gate * jnp.dot(o_ref[rows, :], wbr_ref[...], preferred_element_type=F32)
            if last:
                h_ref[rows, :] = (acc_ref[rows, :] + term).astype(h_ref.dtype)
            elif first:
                acc_ref[rows, :] = term
            else:
                acc_ref[rows, :] += term

    pl.when(j == 0)(lambda: step(True, False))
    pl.when(jnp.logical_and(j > 0, j < n_branch - 1))(lambda: step(False, False))
    pl.when(j == n_branch - 1)(lambda: step(False, True))


def _merge(xb, att, w_gate, w_br, layer, tm, n_chunks):
    rows = xb.shape[0]
    n_branch = 3
    return pl.pallas_call(
        functools.partial(_merge_body, n_branch=n_branch, n_chunks=n_chunks),
        grid=(rows // tm, n_branch),
        in_specs=[
            pl.BlockSpec((tm, D_MODEL), lambda i, j: (i, 0)),
            pl.BlockSpec((tm, A_WIDTH), lambda i, j: (i, j)),
            pl.BlockSpec((None, D_MODEL, D_MODEL), lambda i, j: (layer, 0, j)),
            pl.BlockSpec((None, None, A_WIDTH, D_MODEL), lambda i, j: (layer, j, 0, 0)),
        ],
        out_specs=pl.BlockSpec((tm, D_MODEL), lambda i, j: (i, 0)),
        out_shape=jax.ShapeDtypeStruct((rows, D_MODEL), BF16),
        scratch_shapes=[pltpu.VMEM((tm, D_MODEL), F32)],
        compiler_params=_params(("parallel", "arbitrary"), 52),
        name="merge",
    )(xb, att, w_gate, w_br)


def _outproj_body(x_ref, h_ref, w_ref, g_ref, b_ref, y_ref, yb_ref, *, n_chunks):
    for rows in _row_chunks(x_ref.shape[0], n_chunks):
        mix = jnp.dot(h_ref[rows, :], w_ref[...], preferred_element_type=F32)
        z = DEEPNORM_ALPHA * x_ref[rows, :] + mix
        _store_norm(z, g_ref, b_ref, y_ref, yb_ref, rows)


def _outproj(x, h, w_out, ln_g, ln_b, layer, tm, n_chunks):
    rows = x.shape[0]
    ln_row = layer * 3 + 1
    row_spec = pl.BlockSpec((tm, D_MODEL), lambda i: (i, 0))
    return pl.pallas_call(
        functools.partial(_outproj_body, n_chunks=n_chunks),
        grid=(rows // tm,),
        in_specs=[
            row_spec, row_spec,
            pl.BlockSpec((None, D_MODEL, D_MODEL), lambda i: (layer, 0, 0)),
            pl.BlockSpec((None, 1, D_MODEL), lambda i: (ln_row, 0, 0)),
            pl.BlockSpec((None, 1, D_MODEL), lambda i: (ln_row, 0, 0)),
        ],
        out_specs=[row_spec, row_spec],
        out_shape=[jax.ShapeDtypeStruct((rows, D_MODEL), F32), jax.ShapeDtypeStruct((rows, D_MODEL), BF16)],
        compiler_params=_params(("parallel",), 52),
        name="outproj",
    )(x, h, w_out, ln_g, ln_b)


def _toeplitz(row_of_delta, n_q, n_k):
    period = n_q + n_k
    k = np.arange(period)
    w = row_of_delta(np.where(k < n_k, k, k - period))
    return jnp.tile(w, (1, n_q))[:, :n_q * (period - 1)].reshape(-1, n_q, period - 1)[:, :, :n_k]


def _rel_table_bias(table, offset, n_q, n_k):
    def row(delta):
        idx = np.clip(offset - delta, -(CHUNK - 1), A_REL_MAX) + (CHUNK - 1)
        return table.astype(F32)[:, idx]
    return _toeplitz(row, n_q, n_k)


def _alibi(offset, n_q, n_k):
    slopes = 2.0 ** (-8.0 * (jnp.arange(B_HEADS, dtype=F32) + 1.0) / B_HEADS)
    rel = offset + np.arange(n_q)[:, None] - np.arange(n_k)[None, :]
    return -slopes[:, None, None] * jnp.asarray(np.abs(rel), dtype=F32)


def _band(n_keys, n_prev):
    cq = np.arange(QB)[:, None] // CHUNK
    ck = np.arange(n_keys)[None, :] // CHUNK
    return (ck >= cq) & (ck <= cq + n_prev)


def _prompt_biases(table):
    bias_a = jnp.where(_band(3 * QB, A_PREV_CHUNKS)[None],
                       _rel_table_bias(table, A_PREV_CHUNKS * CHUNK, QB, 3 * QB), NEG_INF)
    nkb = QB // 2 + QB
    bias_b = jnp.where(_band(nkb, B_PREV_CHUNKS)[None], _alibi(B_PREV_CHUNKS * CHUNK, QB, nkb), NEG_INF)
    return bias_a, bias_b


def _sample_biases(table, wa, wb, t):
    ba = _rel_table_bias(table, wa, t, wa + t)
    bb = _alibi(wb, t, wb + t)
    return ba[:, :, :wa], ba[:, :, wa:], bb[:, :, :wb], bb[:, :, wb:]


def kernel(x_prompt, x_sample, cache_a_k, cache_a_v, cache_b_k, cache_b_v, cache_mem_k, cache_mem_v,
           mem_prompt, w_in, w_br_a, w_br_b, w_br_m, w_out, w_mem_kv, rel_bias_a, sink_b,
           ffn1_gu, ffn1_down, ffn2_gu, ffn2_down, ln_g, ln_b):
    batch, seq, _ = x_prompt.shape
    streams, t, _ = x_sample.shape
    a_keep = min(A_PREV_CHUNKS * CHUNK, seq)
    b_keep = min(B_PREV_CHUNKS * CHUNK, seq)
    wa, wb = cache_a_k.shape[2], cache_b_k.shape[2]

    w_qkv = w_in[:, :, :QKV_WIDTH].astype(BF16)
    w_gate = w_in[:, :, QKV_WIDTH:].astype(BF16)
    w_br = jnp.stack([w_br_a, w_br_b, w_br_m], axis=1).astype(BF16)
    w_out_b = w_out.astype(BF16)
    w_mem_b = w_mem_kv.astype(BF16)
    f1gu, f1d = ffn1_gu.astype(BF16), ffn1_down.astype(BF16)
    f2gu, f2d = ffn2_gu.astype(BF16), ffn2_down.astype(BF16)
    lng = ln_g.reshape(DEPTH * 3, 1, D_MODEL)
    lnb = ln_b.reshape(DEPTH * 3, 1, D_MODEL)

    cak = cache_a_k.reshape(DEPTH, streams, wa, A_WIDTH)
    cav = cache_a_v.reshape(DEPTH, streams, wa, A_WIDTH)
    cbk = cache_b_k.reshape(DEPTH, streams, wb, B_KV_WIDTH)
    cbv = cache_b_v.reshape(DEPTH, streams, wb, B_KV_WIDTH)
    cmk = cache_mem_k.reshape(DEPTH, streams, N_MEM, M_WIDTH)
    cmv = cache_mem_v.reshape(DEPTH, streams, N_MEM, M_WIDTH)

    xp = x_prompt.reshape(batch * seq, D_MODEL)
    xs = x_sample.reshape(streams * t, D_MODEL)
    xpb, xsb = xp.astype(BF16), xs.astype(BF16)
    memb = mem_prompt.reshape(batch * N_MEM, D_MODEL).astype(BF16)

    tm_p, ch_p = 512, 2
    tm_s = streams * t
    n_small_qkv = QKV_WIDTH // TN_SMALL
    n_small_mem = 2 * M_WIDTH // TN_SMALL
    kv_first, kv_b = COL_KA // TN_SMALL, COL_KB // TN_SMALL
    n_kv_a = 2 * A_WIDTH // TN_SMALL

    akp, avp, bkp, bvp, mkp, mvp = [], [], [], [], [], []
    aks, avs, bks, bvs = [], [], [], []
    for l in range(DEPTH):
        bias_a, bias_b = _prompt_biases(rel_bias_a[l])
        s_biases = _sample_biases(rel_bias_a[l], wa, wb, t)
        sink = sink_b[l].astype(F32)

        xp, xpb = _ffn(xp, xpb, f1gu, f1d, lng, lnb, l, 0, tm_p, ch_p)
        p = _proj(xpb, w_qkv, l, TN_QKV, lambda j: j, QKV_WIDTH // TN_QKV, 1024, 4, BF16)
        tail = xpb.reshape(batch, seq, D_MODEL)[:, seq - a_keep:].reshape(batch * a_keep, D_MODEL)
        kv = _proj(tail, w_qkv, l, TN_SMALL, lambda j: jnp.where(j < n_kv_a, j + kv_first, kv_b),
                   n_kv_a + 1, 512, 1, F32).reshape(batch, a_keep, -1)
        mkv = _proj(memb, w_mem_b, l, TN_SMALL, lambda j: j, n_small_mem, batch * N_MEM, 1, F32)
        mk, mv = mkv[:, :M_WIDTH], mkv[:, M_WIDTH:]
        att = _attn_prompt(p, mk.astype(BF16), mv.astype(BF16), bias_a, bias_b, sink, batch, seq)
        h = _merge(xpb, att, w_gate, w_br, l, tm_p, ch_p)
        xp, xpb = _outproj(xp, h, w_out_b, lng, lnb, l, tm_p, ch_p)
        xp, xpb = _ffn(xp, xpb, f2gu, f2d, lng, lnb, l, 2, tm_p, ch_p)

        akp.append(kv[:, :, :A_WIDTH].reshape(batch, a_keep, A_HEADS, HEAD_DIM))
        avp.append(kv[:, :, A_WIDTH:2 * A_WIDTH].reshape(batch, a_keep, A_HEADS, HEAD_DIM))
        kvb = kv[:, a_keep - b_keep:, 2 * A_WIDTH:]
        bkp.append(kvb[:, :, :B_KV_WIDTH].reshape(batch, b_keep, B_KV_HEADS, HEAD_DIM))
        bvp.append(kvb[:, :, B_KV_WIDTH:].reshape(batch, b_keep, B_KV_HEADS, HEAD_DIM))
        mkp.append(mk.reshape(batch, N_MEM, M_HEADS, M_HEAD_DIM))
        mvp.append(mv.reshape(batch, N_MEM, M_HEADS, M_HEAD_DIM))

        xs, xsb = _ffn(xs, xsb, f1gu, f1d, lng, lnb, l, 0, tm_s, 1)
        ps = _proj(xsb, w_qkv, l, TN_SMALL, lambda j: j, n_small_qkv, tm_s, 1, F32)
        att_s = _attn_sample(ps, cak, cav, cbk, cbv, cmk, cmv, s_biases, sink, l, streams, t)
        hs = _merge(xsb, att_s, w_gate, w_br, l, tm_s, 1)
        xs, xsb = _outproj(xs, hs, w_out_b, lng, lnb, l, tm_s, 1)
        xs, xsb = _ffn(xs, xsb, f2gu, f2d, lng, lnb, l, 2, tm_s, 1)

        ps3 = ps.reshape(streams, t, QKV_WIDTH)
        aks.append(ps3[:, :, COL_KA:COL_KA + A_WIDTH].reshape(streams, t, A_HEADS, HEAD_DIM))
        avs.append(ps3[:, :, COL_VA:COL_VA + A_WIDTH].reshape(streams, t, A_HEADS, HEAD_DIM))
        bks.append(ps3[:, :, COL_KB:COL_KB + B_KV_WIDTH].reshape(streams, t, B_KV_HEADS, HEAD_DIM))
        bvs.append(ps3[:, :, COL_VB:COL_VB + B_KV_WIDTH].reshape(streams, t, B_KV_HEADS, HEAD_DIM))

    return (xp.reshape(batch, seq, D_MODEL), xs.reshape(streams, t, D_MODEL),
            jnp.stack(akp), jnp.stack(avp), jnp.stack(bkp), jnp.stack(bvp), jnp.stack(mkp), jnp.stack(mvp),
            jnp.stack(aks), jnp.stack(avs), jnp.stack(bks), jnp.stack(bvs))
```

```python
import functools

import numpy as np
import jax
import jax.numpy as jnp
from jax import lax
from jax.experimental import pallas as pl
from jax.experimental.pallas import tpu as pltpu

F32 = jnp.float32
BF16 = jnp.bfloat16

D_MODEL = 2048
DEPTH = 4
CHUNK = 64
HEAD_DIM = 128
A_HEADS = 8
A_PREV_CHUNKS = 8
A_REL_MAX = 128
B_HEADS = 8
B_KV_HEADS = 2
B_PREV_CHUNKS = 2
M_HEADS = 4
M_HEAD_DIM = 256
N_MEM = 256
D_FF = 5632

A_WIDTH = A_HEADS * HEAD_DIM
B_Q_WIDTH = B_HEADS * HEAD_DIM
B_KV_WIDTH = B_KV_HEADS * HEAD_DIM
M_WIDTH = M_HEADS * M_HEAD_DIM
QKV_WIDTH = 3 * A_WIDTH + B_Q_WIDTH + 2 * B_KV_WIDTH + M_WIDTH
COL_QA, COL_KA, COL_VA = 0, A_WIDTH, 2 * A_WIDTH
COL_QB = 3 * A_WIDTH
COL_KB = COL_QB + B_Q_WIDTH
COL_VB = COL_KB + B_KV_WIDTH
COL_QM = COL_VB + B_KV_WIDTH
ATT_WIDTH = A_WIDTH + B_Q_WIDTH + M_WIDTH

DEEPNORM_ALPHA = (2 * DEPTH) ** 0.25
LN_EPS = 1e-5
NEG_INF = -1e30

MIB = 1024 * 1024
QB = 4 * CHUNK
TF = 512
TN_QKV = QKV_WIDTH // 2
TN_SMALL = 512


def _params(sem, vmem_mib):
    return pltpu.CompilerParams(dimension_semantics=sem, vmem_limit_bytes=vmem_mib * MIB)


def _layernorm(z, g, b):
    mu = jnp.mean(z, axis=-1, keepdims=True)
    zc = z - mu
    var = jnp.mean(zc * zc, axis=-1, keepdims=True)
    return zc * lax.rsqrt(var + LN_EPS) * g + b


def _row_chunks(tm, n_chunks):
    rc = tm // n_chunks
    return [slice(r * rc, (r + 1) * rc) for r in range(n_chunks)]


def _store_norm(z, g_ref, b_ref, y_ref, yb_ref, rows):
    y = _layernorm(z, g_ref[...], b_ref[...])
    y_ref[rows, :] = y
    yb_ref[rows, :] = y.astype(BF16)


def _ffn_tile(xb, wgu, wd):
    gu = jnp.dot(xb, wgu, preferred_element_type=F32)
    gate, up = gu[:, :TF], gu[:, TF:]
    h = (gate * jax.nn.sigmoid(gate) * up).astype(BF16)
    return jnp.dot(h, wd, preferred_element_type=F32)


def _ffn_body(x_ref, wgu_ref, wd_ref, g_ref, b_ref, y_ref, yb_ref, xb_ref, acc_ref, *, nj):
    j = pl.program_id(1)

    @pl.when(j == 0)
    def _():
        xb_ref[...] = x_ref[...].astype(BF16)
        acc_ref[...] = jnp.zeros_like(acc_ref)

    acc_ref[...] += _ffn_tile(xb_ref[...], wgu_ref[...], wd_ref[...])

    @pl.when(j == nj - 1)
    def _():
        z = DEEPNORM_ALPHA * x_ref[...] + 0.5 * acc_ref[...]
        _store_norm(z, g_ref, b_ref, y_ref, yb_ref, slice(None))


def _ffn(x, w_gu_tiles, w_down_tiles, ln_g, ln_b, layer, ln_idx, tm):
    rows = x.shape[0]
    nj = D_FF // TF
    ln_row = layer * 3 + ln_idx
    row_spec = pl.BlockSpec((tm, D_MODEL), lambda i, j: (i, 0))
    ln_spec = pl.BlockSpec((None, 1, D_MODEL), lambda i, j: (ln_row, 0, 0))
    return pl.pallas_call(
        functools.partial(_ffn_body, nj=nj),
        grid=(rows // tm, nj),
        in_specs=[
            row_spec,
            pl.BlockSpec((None, None, D_MODEL, 2 * TF), lambda i, j: (layer, j, 0, 0)),
            pl.BlockSpec((None, None, TF, D_MODEL), lambda i, j: (layer, j, 0, 0)),
            ln_spec, ln_spec,
        ],
        out_specs=[row_spec, row_spec],
        out_shape=[jax.ShapeDtypeStruct((rows, D_MODEL), F32), jax.ShapeDtypeStruct((rows, D_MODEL), BF16)],
        scratch_shapes=[pltpu.VMEM((tm, D_MODEL), BF16), pltpu.VMEM((tm, D_MODEL), F32)],
        compiler_params=_params(("parallel", "arbitrary"), 52),
        name="ffn",
    )(x, w_gu_tiles, w_down_tiles, ln_g, ln_b)


def _ffn_stream_body(x_ref, wgu_hbm, wd_hbm, g_ref, b_ref, y_ref, yb_ref,
                     first_gu, first_d, ring_gu, ring_d, first_sem, ring_sem, xb_ref, acc_ref,
                     *, layer, nj, n_chunks):
    i = pl.program_id(0)
    n_pairs = (nj - 1) // 2
    chunks = _row_chunks(x_ref.shape[0], n_chunks)

    def ring_copies(j, slot):
        return (pltpu.make_async_copy(wgu_hbm.at[layer, j], ring_gu.at[slot], ring_sem.at[0, slot]),
                pltpu.make_async_copy(wd_hbm.at[layer, j], ring_d.at[slot], ring_sem.at[1, slot]))

    @pl.when(i == 0)
    def _():
        copies = (pltpu.make_async_copy(wgu_hbm.at[layer, 0], first_gu, first_sem.at[0]),
                  pltpu.make_async_copy(wd_hbm.at[layer, 0], first_d, first_sem.at[1]))
        for c in copies + ring_copies(1, 0):
            c.start()
        for c in copies:
            c.wait()

    for rows in chunks:
        xb = x_ref[rows, :].astype(BF16)
        xb_ref[rows, :] = xb
        acc_ref[rows, :] = _ffn_tile(xb, first_gu[...], first_d[...])

    def pair(k, carry):
        j = 2 * k + 1
        for c in ring_copies(j, 0):
            c.wait()
        for c in ring_copies(j + 1, 1):
            c.start()
        for rows in chunks:
            acc_ref[rows, :] += _ffn_tile(xb_ref[rows, :], ring_gu[0], ring_d[0])
        for c in ring_copies(j + 1, 1):
            c.wait()
        nxt = jnp.where(k < n_pairs - 1, j + 2, 1)
        for c in ring_copies(nxt, 0):
            c.start()
        for rows in chunks:
            acc_ref[rows, :] += _ffn_tile(xb_ref[rows, :], ring_gu[1], ring_d[1])
        return carry

    lax.fori_loop(0, n_pairs, pair, 0)

    for rows in chunks:
        z = DEEPNORM_ALPHA * x_ref[rows, :] + 0.5 * acc_ref[rows, :]
        _store_norm(z, g_ref, b_ref, y_ref, yb_ref, rows)

    @pl.when(i == pl.num_programs(0) - 1)
    def _():
        for c in ring_copies(1, 0):
            c.wait()


def _ffn_stream(x, w_gu_tiles, w_down_tiles, ln_g, ln_b, layer, ln_idx, tm, n_chunks):
    rows = x.shape[0]
    nj = D_FF // TF
    assert nj % 2 == 1
    ln_row = layer * 3 + ln_idx
    row_spec = pl.BlockSpec((tm, D_MODEL), lambda i: (i, 0))
    ln_spec = pl.BlockSpec((None, 1, D_MODEL), lambda i: (ln_row, 0, 0))
    return pl.pallas_call(
        functools.partial(_ffn_stream_body, layer=layer, nj=nj, n_chunks=n_chunks),
        grid=(rows // tm,),
        in_specs=[row_spec, pl.BlockSpec(memory_space=pl.ANY), pl.BlockSpec(memory_space=pl.ANY),
                  ln_spec, ln_spec],
        out_specs=[row_spec, row_spec],
        out_shape=[jax.ShapeDtypeStruct((rows, D_MODEL), F32), jax.ShapeDtypeStruct((rows, D_MODEL), BF16)],
        scratch_shapes=[
            pltpu.VMEM((D_MODEL, 2 * TF), BF16), pltpu.VMEM((TF, D_MODEL), BF16),
            pltpu.VMEM((2, D_MODEL, 2 * TF), BF16), pltpu.VMEM((2, TF, D_MODEL), BF16),
            pltpu.SemaphoreType.DMA((2,)), pltpu.SemaphoreType.DMA((2, 2)),
            pltpu.VMEM((tm, D_MODEL), BF16), pltpu.VMEM((tm, D_MODEL), F32),
        ],
        compiler_params=_params(("arbitrary",), 56),
        name="ffn_stream",
    )(x, w_gu_tiles, w_down_tiles, ln_g, ln_b)


def _proj_body(xb_ref, w_ref, o_ref, *, n_chunks):
    for rows in _row_chunks(xb_ref.shape[0], n_chunks):
        o_ref[rows, :] = jnp.dot(xb_ref[rows, :], w_ref[...], preferred_element_type=F32).astype(o_ref.dtype)


def _proj(xb, w, layer, tn, col_block, n_col_blocks, tm, n_chunks, out_dtype):
    rows = xb.shape[0]
    return pl.pallas_call(
        functools.partial(_proj_body, n_chunks=n_chunks),
        grid=(rows // tm, n_col_blocks),
        in_specs=[
            pl.BlockSpec((tm, D_MODEL), lambda i, j: (i, 0)),
            pl.BlockSpec((None, D_MODEL, tn), lambda i, j: (layer, 0, col_block(j))),
        ],
        out_specs=pl.BlockSpec((tm, tn), lambda i, j: (i, j)),
        out_shape=jax.ShapeDtypeStruct((rows, n_col_blocks * tn), out_dtype),
        compiler_params=_params(("parallel", "arbitrary"), 52),
        name="proj",
    )(xb, w)


def _qk(q, k):
    return lax.dot_general(q, k, (((1,), (1,)), ((), ())), preferred_element_type=F32)


def _softmax_pv(scores, values, sink=None):
    m = scores[0].max(axis=-1, keepdims=True)
    for s in scores[1:]:
        m = jnp.maximum(m, s.max(axis=-1, keepdims=True))
    if sink is not None:
        m = jnp.maximum(m, sink)
    es = [jnp.exp(s - m) for s in scores]
    denom = es[0].sum(axis=-1, keepdims=True)
    for e in es[1:]:
        denom = denom + e.sum(axis=-1, keepdims=True)
    if sink is not None:
        denom = denom + jnp.exp(sink - m)
    o = jnp.dot(es[0].astype(BF16), values[0], preferred_element_type=F32)
    for e, v in zip(es[1:], values[1:]):
        o = o + jnp.dot(e.astype(BF16), v, preferred_element_type=F32)
    return o / denom


def _hs(h, width=HEAD_DIM):
    return slice(h * width, (h + 1) * width)


def _attn_prompt_body(sink_ref, cur_ref, ka2_ref, ka1_ref, va2_ref, va1_ref, kbp_ref, vbp_ref,
                      mk_ref, mv_ref, bias_a_ref, bias_b_ref, o_ref):
    i = pl.program_id(1)
    scale = F32(HEAD_DIM ** -0.5)
    rep = B_HEADS // B_KV_HEADS
    half = QB // 2

    def cur(c0, h, width=HEAD_DIM):
        return cur_ref[:, c0 + h * width:c0 + (h + 1) * width]

    pen2 = jnp.where(i >= 2, F32(0.0), F32(NEG_INF))
    pen1 = jnp.where(i >= 1, F32(0.0), F32(NEG_INF))

    for h in range(A_HEADS):
        q = cur(COL_QA, h)
        s2 = _qk(q, ka2_ref[:, _hs(h)]) * scale + bias_a_ref[h, :, 0:QB] + pen2
        s1 = _qk(q, ka1_ref[:, _hs(h)]) * scale + bias_a_ref[h, :, QB:2 * QB] + pen1
        s0 = _qk(q, cur(COL_KA, h)) * scale + bias_a_ref[h, :, 2 * QB:3 * QB]
        vs = [va2_ref[:, _hs(h)], va1_ref[:, _hs(h)], cur(COL_VA, h)]
        o_ref[:, _hs(h)] = _softmax_pv([s2, s1, s0], vs).astype(o_ref.dtype)

    for h in range(B_HEADS):
        g = h // rep
        q = cur(COL_QB, h)
        s1 = _qk(q, kbp_ref[:, _hs(g)]) * scale + bias_b_ref[h, :, 0:half] + pen1
        s0 = _qk(q, cur(COL_KB, g)) * scale + bias_b_ref[h, :, half:half + QB]
        o = _softmax_pv([s1, s0], [vbp_ref[:, _hs(g)], cur(COL_VB, g)], sink=sink_ref[h])
        o_ref[:, A_WIDTH + h * HEAD_DIM:A_WIDTH + (h + 1) * HEAD_DIM] = o.astype(o_ref.dtype)

    mscale = F32(M_HEAD_DIM ** -0.5)
    for h in range(M_HEADS):
        s = _qk(cur(COL_QM, h, M_HEAD_DIM), mk_ref[:, _hs(h, M_HEAD_DIM)]) * mscale
        o = _softmax_pv([s], [mv_ref[:, _hs(h, M_HEAD_DIM)]])
        base = A_WIDTH + B_Q_WIDTH
        o_ref[:, base + h * M_HEAD_DIM:base + (h + 1) * M_HEAD_DIM] = o.astype(o_ref.dtype)


def _attn_prompt(p, mk, mv, bias_a, bias_b, sink, batch, seq):
    nb = seq // QB
    half = QB // 2
    ka_blk, va_blk = COL_KA // A_WIDTH, COL_VA // A_WIDTH
    kb_blk, vb_blk = COL_KB // B_KV_WIDTH, COL_VB // B_KV_WIDTH

    def prev_a(col, back):
        return pl.BlockSpec((QB, A_WIDTH), lambda b, i: (b * nb + jnp.maximum(i - back, 0), col))

    def prev_b(col):
        return pl.BlockSpec((half, B_KV_WIDTH),
                            lambda b, i: (b * 2 * nb + jnp.maximum(2 * i - 1, 0), col))

    return pl.pallas_call(
        _attn_prompt_body,
        grid=(batch, nb),
        in_specs=[
            pl.BlockSpec(memory_space=pltpu.SMEM),
            pl.BlockSpec((QB, QKV_WIDTH), lambda b, i: (b * nb + i, 0)),
            prev_a(ka_blk, 2), prev_a(ka_blk, 1), prev_a(va_blk, 2), prev_a(va_blk, 1),
            prev_b(kb_blk), prev_b(vb_blk),
            pl.BlockSpec((N_MEM, M_WIDTH), lambda b, i: (b, 0)),
            pl.BlockSpec((N_MEM, M_WIDTH), lambda b, i: (b, 0)),
            pl.BlockSpec((A_HEADS, QB, 3 * QB), lambda b, i: (0, 0, 0)),
            pl.BlockSpec((B_HEADS, QB, half + QB), lambda b, i: (0, 0, 0)),
        ],
        out_specs=pl.BlockSpec((QB, ATT_WIDTH), lambda b, i: (b * nb + i, 0)),
        out_shape=jax.ShapeDtypeStruct((batch * seq, ATT_WIDTH), BF16),
        compiler_params=_params(("parallel", "arbitrary"), 48),
        name="attn_prompt",
    )(sink, p, p, p, p, p, p, p, mk, mv, bias_a, bias_b)


def _attn_sample_body(sink_ref, cur_ref, cak_ref, cav_ref, cbk_ref, cbv_ref, cmk_ref, cmv_ref,
                      bias_ac_ref, bias_an_ref, bias_bc_ref, bias_bn_ref, o_ref):
    scale = F32(HEAD_DIM ** -0.5)

    def col(c0, h, width=HEAD_DIM):
        return cur_ref[:, c0 + h * width:c0 + (h + 1) * width].astype(BF16)

    for h in range(A_HEADS):
        q = col(COL_QA, h)
        ks = [cak_ref[:, _hs(h)].astype(BF16), col(COL_KA, h)]
        vs = [cav_ref[:, _hs(h)].astype(BF16), col(COL_VA, h)]
        sc = _qk(q, ks[0]) * scale + bias_ac_ref[h]
        sn = _qk(q, ks[1]) * scale + bias_an_ref[h]
        o_ref[:, _hs(h)] = _softmax_pv([sc, sn], vs).astype(o_ref.dtype)

    rep = B_HEADS // B_KV_HEADS
    for h in range(B_HEADS):
        g = h // rep
        q = col(COL_QB, h)
        ks = [cbk_ref[:, _hs(g)].astype(BF16), col(COL_KB, g)]
        vs = [cbv_ref[:, _hs(g)].astype(BF16), col(COL_VB, g)]
        sc = _qk(q, ks[0]) * scale + bias_bc_ref[h]
        sn = _qk(q, ks[1]) * scale + bias_bn_ref[h]
        o = _softmax_pv([sc, sn], vs, sink=sink_ref[h])
        o_ref[:, A_WIDTH + h * HEAD_DIM:A_WIDTH + (h + 1) * HEAD_DIM] = o.astype(o_ref.dtype)

    mscale = F32(M_HEAD_DIM ** -0.5)
    for h in range(M_HEADS):
        q = col(COL_QM, h, M_HEAD_DIM)
        s = _qk(q, cmk_ref[:, _hs(h, M_HEAD_DIM)].astype(BF16)) * mscale
        o = _softmax_pv([s], [cmv_ref[:, _hs(h, M_HEAD_DIM)].astype(BF16)])
        base = A_WIDTH + B_Q_WIDTH
        o_ref[:, base + h * M_HEAD_DIM:base + (h + 1) * M_HEAD_DIM] = o.astype(o_ref.dtype)


def _attn_sample(p, cak, cav, cbk, cbv, cmk, cmv, biases, sink, layer, streams, t):
    bias_ac, bias_an, bias_bc, bias_bn = biases

    def cache_spec(c):
        return pl.BlockSpec((None, None, c.shape[2], c.shape[3]), lambda s: (layer, s, 0, 0))

    def full_spec(a):
        return pl.BlockSpec(a.shape, lambda s: (0, 0, 0))

    return pl.pallas_call(
        _attn_sample_body,
        grid=(streams,),
        in_specs=[
            pl.BlockSpec(memory_space=pltpu.SMEM),
            pl.BlockSpec((t, QKV_WIDTH), lambda s: (s, 0)),
            cache_spec(cak), cache_spec(cav), cache_spec(cbk), cache_spec(cbv),
            cache_spec(cmk), cache_spec(cmv),
            full_spec(bias_ac), full_spec(bias_an), full_spec(bias_bc), full_spec(bias_bn),
        ],
        out_specs=pl.BlockSpec((t, ATT_WIDTH), lambda s: (s, 0)),
        out_shape=jax.ShapeDtypeStruct((streams * t, ATT_WIDTH), BF16),
        compiler_params=_params(("arbitrary",), 32),
        name="attn_sample",
    )(sink, p, cak, cav, cbk, cbv, cmk, cmv, bias_ac, bias_an, bias_bc, bias_bn)


def _merge_body(xb_ref, o_ref, wg_ref, wbr_ref, h_ref, acc_ref, *, n_branch, n_chunks):
    j = pl.program_id(1)

    def step(first, last):
        for rows in _row_chunks(xb_ref.shape[0], n_chunks):
            gate = jax.nn.sigmoid(jnp.dot(xb_ref[rows, :], wg_ref[...], preferred_element_type=F32))
            term = gate * jnp.dot(o_ref[rows, :], wbr_ref[...], preferred_element_type=F32)
            if last:
                h_ref[rows, :] = (acc_ref[rows, :] + term).astype(h_ref.dtype)
            elif first:
                acc_ref[rows, :] = term
            else:
                acc_ref[rows, :] += term

    pl.when(j == 0)(lambda: step(True, False))
    pl.when(jnp.logical_and(j > 0, j < n_branch - 1))(lambda: step(False, False))
    pl.when(j == n_branch - 1)(lambda: step(False, True))


def _merge(xb, att, w_gate, w_br, layer, tm, n_chunks):
    rows = xb.shape[0]
    n_branch = 3
    return pl.pallas_call(
        functools.partial(_merge_body, n_branch=n_branch, n_chunks=n_chunks),
        grid=(rows // tm, n_branch),
        in_specs=[
            pl.BlockSpec((tm, D_MODEL), lambda i, j: (i, 0)),
            pl.BlockSpec((tm, A_WIDTH), lambda i, j: (i, j)),
            pl.BlockSpec((None, D_MODEL, D_MODEL), lambda i, j: (layer, 0, j)),
            pl.BlockSpec((None, None, A_WIDTH, D_MODEL), lambda i, j: (layer, j, 0, 0)),
        ],
        out_specs=pl.BlockSpec((tm, D_MODEL), lambda i, j: (i, 0)),
        out_shape=jax.ShapeDtypeStruct((rows, D_MODEL), BF16),
        scratch_shapes=[pltpu.VMEM((tm, D_MODEL), F32)],
        compiler_params=_params(("parallel", "arbitrary"), 52),
        name="merge",
    )(xb, att, w_gate, w_br)


def _outproj_body(x_ref, h_ref, w_ref, g_ref, b_ref, y_ref, yb_ref, *, n_chunks):
    for rows in _row_chunks(x_ref.shape[0], n_chunks):
        mix = jnp.dot(h_ref[rows, :], w_ref[...], preferred_element_type=F32)
        z = DEEPNORM_ALPHA * x_ref[rows, :] + mix
        _store_norm(z, g_ref, b_ref, y_ref, yb_ref, rows)


def _outproj(x, h, w_out, ln_g, ln_b, layer, tm, n_chunks):
    rows = x.shape[0]
    ln_row = layer * 3 + 1
    row_spec = pl.BlockSpec((tm, D_MODEL), lambda i: (i, 0))
    return pl.pallas_call(
        functools.partial(_outproj_body, n_chunks=n_chunks),
        grid=(rows // tm,),
        in_specs=[
            row_spec, row_spec,
            pl.BlockSpec((None, D_MODEL, D_MODEL), lambda i: (layer, 0, 0)),
            pl.BlockSpec((None, 1, D_MODEL), lambda i: (ln_row, 0, 0)),
            pl.BlockSpec((None, 1, D_MODEL), lambda i: (ln_row, 0, 0)),
        ],
        out_specs=[row_spec, row_spec],
        out_shape=[jax.ShapeDtypeStruct((rows, D_MODEL), F32), jax.ShapeDtypeStruct((rows, D_MODEL), BF16)],
        compiler_params=_params(("parallel",), 52),
        name="outproj",
    )(x, h, w_out, ln_g, ln_b)


def _toeplitz(row_of_delta, n_q, n_k):
    period = n_q + n_k
    k = np.arange(period)
    w = row_of_delta(np.where(k < n_k, k, k - period))
    return jnp.tile(w, (1, n_q))[:, :n_q * (period - 1)].reshape(-1, n_q, period - 1)[:, :, :n_k]


def _rel_table_bias(table, offset, n_q, n_k):
    def row(delta):
        idx = np.clip(offset - delta, -(CHUNK - 1), A_REL_MAX) + (CHUNK - 1)
        return table.astype(F32)[:, idx]
    return _toeplitz(row, n_q, n_k)


def _alibi(offset, n_q, n_k):
    slopes = 2.0 ** (-8.0 * (jnp.arange(B_HEADS, dtype=F32) + 1.0) / B_HEADS)
    rel = offset + np.arange(n_q)[:, None] - np.arange(n_k)[None, :]
    return -slopes[:, None, None] * jnp.asarray(np.abs(rel), dtype=F32)


def _band(n_keys, n_prev):
    cq = np.arange(QB)[:, None] // CHUNK
    ck = np.arange(n_keys)[None, :] // CHUNK
    return (ck >= cq) & (ck <= cq + n_prev)


def _prompt_biases(table):
    bias_a = jnp.where(_band(3 * QB, A_PREV_CHUNKS)[None],
                       _rel_table_bias(table, A_PREV_CHUNKS * CHUNK, QB, 3 * QB), NEG_INF)
    nkb = QB // 2 + QB
    bias_b = jnp.where(_band(nkb, B_PREV_CHUNKS)[None], _alibi(B_PREV_CHUNKS * CHUNK, QB, nkb), NEG_INF)
    return bias_a, bias_b


def _sample_biases(table, wa, wb, t):
    ba = _rel_table_bias(table, wa, t, wa + t)
    bb = _alibi(wb, t, wb + t)
    return ba[:, :, :wa], ba[:, :, wa:], bb[:, :, :wb], bb[:, :, wb:]


def kernel(x_prompt, x_sample, cache_a_k, cache_a_v, cache_b_k, cache_b_v, cache_mem_k, cache_mem_v,
           mem_prompt, w_in, w_br_a, w_br_b, w_br_m, w_out, w_mem_kv, rel_bias_a, sink_b,
           ffn1_gu, ffn1_down, ffn2_gu, ffn2_down, ln_g, ln_b):
    batch, seq, _ = x_prompt.shape
    streams, t, _ = x_sample.shape
    a_keep = min(A_PREV_CHUNKS * CHUNK, seq)
    b_keep = min(B_PREV_CHUNKS * CHUNK, seq)
    wa, wb = cache_a_k.shape[2], cache_b_k.shape[2]

    w_qkv = w_in[:, :, :QKV_WIDTH].astype(BF16)
    w_gate = w_in[:, :, QKV_WIDTH:].astype(BF16)
    w_br = jnp.stack([w_br_a, w_br_b, w_br_m], axis=1).astype(BF16)
    w_out_b = w_out.astype(BF16)
    w_mem_b = w_mem_kv.astype(BF16)
    nj = D_FF // TF

    def gu_tiles(w):
        w = w.reshape(DEPTH, D_MODEL, 2, nj, TF).transpose(0, 3, 1, 2, 4)
        return w.reshape(DEPTH, nj, D_MODEL, 2 * TF).astype(BF16)

    def down_tiles(w):
        return w.reshape(DEPTH, nj, TF, D_MODEL).astype(BF16)

    f1gu, f1d = gu_tiles(ffn1_gu), down_tiles(ffn1_down)
    f2gu, f2d = gu_tiles(ffn2_gu), down_tiles(ffn2_down)
    lng = ln_g.reshape(DEPTH * 3, 1, D_MODEL)
    lnb = ln_b.reshape(DEPTH * 3, 1, D_MODEL)

    cak = cache_a_k.reshape(DEPTH, streams, wa, A_WIDTH)
    cav = cache_a_v.reshape(DEPTH, streams, wa, A_WIDTH)
    cbk = cache_b_k.reshape(DEPTH, streams, wb, B_KV_WIDTH)
    cbv = cache_b_v.reshape(DEPTH, streams, wb, B_KV_WIDTH)
    cmk = cache_mem_k.reshape(DEPTH, streams, N_MEM, M_WIDTH)
    cmv = cache_mem_v.reshape(DEPTH, streams, N_MEM, M_WIDTH)

    xp = x_prompt.reshape(batch * seq, D_MODEL)
    xs = x_sample.reshape(streams * t, D_MODEL)
    memb = mem_prompt.reshape(batch * N_MEM, D_MODEL).astype(BF16)

    tm_p, ch_p = 512, 2
    tm_s = streams * t
    n_small_qkv = QKV_WIDTH // TN_SMALL
    n_small_mem = 2 * M_WIDTH // TN_SMALL
    kv_first, kv_b = COL_KA // TN_SMALL, COL_KB // TN_SMALL
    n_kv_a = 2 * A_WIDTH // TN_SMALL

    akp, avp, bkp, bvp, mkp, mvp = [], [], [], [], [], []
    aks, avs, bks, bvs = [], [], [], []
    for l in range(DEPTH):
        bias_a, bias_b = _prompt_biases(rel_bias_a[l])
        s_biases = _sample_biases(rel_bias_a[l], wa, wb, t)
        sink = sink_b[l].astype(F32)

        xp, xpb = _ffn_stream(xp, f1gu, f1d, lng, lnb, l, 0, tm_p, ch_p)
        p = _proj(xpb, w_qkv, l, TN_QKV, lambda j: j, QKV_WIDTH // TN_QKV, 1024, 4, BF16)
        tail = xpb.reshape(batch, seq, D_MODEL)[:, seq - a_keep:].reshape(batch * a_keep, D_MODEL)
        kv = _proj(tail, w_qkv, l, TN_SMALL, lambda j: jnp.where(j < n_kv_a, j + kv_first, kv_b),
                   n_kv_a + 1, 512, 1, F32).reshape(batch, a_keep, -1)
        mkv = _proj(memb, w_mem_b, l, TN_SMALL, lambda j: j, n_small_mem, batch * N_MEM, 1, F32)
        mk, mv = mkv[:, :M_WIDTH], mkv[:, M_WIDTH:]
        att = _attn_prompt(p, mk.astype(BF16), mv.astype(BF16), bias_a, bias_b, sink, batch, seq)
        h = _merge(xpb, att, w_gate, w_br, l, tm_p, ch_p)
        xp, _ = _outproj(xp, h, w_out_b, lng, lnb, l, tm_p, ch_p)
        xp, _ = _ffn(xp, f2gu, f2d, lng, lnb, l, 2, tm_p)

        akp.append(kv[:, :, :A_WIDTH].reshape(batch, a_keep, A_HEADS, HEAD_DIM))
        avp.append(kv[:, :, A_WIDTH:2 * A_WIDTH].reshape(batch, a_keep, A_HEADS, HEAD_DIM))
        kvb = kv[:, a_keep - b_keep:, 2 * A_WIDTH:]
        bkp.append(kvb[:, :, :B_KV_WIDTH].reshape(batch, b_keep, B_KV_HEADS, HEAD_DIM))
        bvp.append(kvb[:, :, B_KV_WIDTH:].reshape(batch, b_keep, B_KV_HEADS, HEAD_DIM))
        mkp.append(mk.reshape(batch, N_MEM, M_HEADS, M_HEAD_DIM))
        mvp.append(mv.reshape(batch, N_MEM, M_HEADS, M_HEAD_DIM))

        xs, xsb = _ffn(xs, f1gu, f1d, lng, lnb, l, 0, tm_s)
        ps = _proj(xsb, w_qkv, l, TN_SMALL, lambda j: j, n_small_qkv, tm_s, 1, F32)
        att_s = _attn_sample(ps, cak, cav, cbk, cbv, cmk, cmv, s_biases, sink, l, streams, t)
        hs = _merge(xsb, att_s, w_gate, w_br, l, tm_s, 1)
        xs, _ = _outproj(xs, hs, w_out_b, lng, lnb, l, tm_s, 1)
        xs, _ = _ffn(xs, f2gu, f2d, lng, lnb, l, 2, tm_s)

        ps3 = ps.reshape(streams, t, QKV_WIDTH)
        aks.append(ps3[:, :, COL_KA:COL_KA + A_WIDTH].reshape(streams, t, A_HEADS, HEAD_DIM))
        avs.append(ps3[:, :, COL_VA:COL_VA + A_WIDTH].reshape(streams, t, A_HEADS, HEAD_DIM))
        bks.append(ps3[:, :, COL_KB:COL_KB + B_KV_WIDTH].reshape(streams, t, B_KV_HEADS, HEAD_DIM))
        bvs.append(ps3[:, :, COL_VB:COL_VB + B_KV_WIDTH].reshape(streams, t, B_KV_HEADS, HEAD_DIM))

    return (xp.reshape(batch, seq, D_MODEL), xs.reshape(streams, t, D_MODEL),
            jnp.stack(akp), jnp.stack(avp), jnp.stack(bkp), jnp.stack(bvp), jnp.stack(mkp), jnp.stack(mvp),
            jnp.stack(aks), jnp.stack(avs), jnp.stack(bks), jnp.stack(bvs))
```

```python
import functools

import numpy as np
import jax
import jax.numpy as jnp
from jax import lax
from jax.experimental import pallas as pl
from jax.experimental.pallas import tpu as pltpu

F32 = jnp.float32
BF16 = jnp.bfloat16

D_MODEL = 2048
DEPTH = 4
CHUNK = 64
HEAD_DIM = 128
A_HEADS = 8
A_PREV_CHUNKS = 8
A_REL_MAX = 128
B_HEADS = 8
B_KV_HEADS = 2
B_PREV_CHUNKS = 2
M_HEADS = 4
M_HEAD_DIM = 256
N_MEM = 256
D_FF = 5632

A_WIDTH = A_HEADS * HEAD_DIM
B_Q_WIDTH = B_HEADS * HEAD_DIM
B_KV_WIDTH = B_KV_HEADS * HEAD_DIM
M_WIDTH = M_HEADS * M_HEAD_DIM
QKV_WIDTH = 3 * A_WIDTH + B_Q_WIDTH + 2 * B_KV_WIDTH + M_WIDTH
COL_QA, COL_KA, COL_VA = 0, A_WIDTH, 2 * A_WIDTH
COL_QB = 3 * A_WIDTH
COL_KB = COL_QB + B_Q_WIDTH
COL_VB = COL_KB + B_KV_WIDTH
COL_QM = COL_VB + B_KV_WIDTH
ATT_WIDTH = A_WIDTH + B_Q_WIDTH + M_WIDTH

DEEPNORM_ALPHA = (2 * DEPTH) ** 0.25
LN_EPS = 1e-5
NEG_INF = -1e30

MIB = 1024 * 1024
QB = 4 * CHUNK
TF = 512
TN_QKV = QKV_WIDTH // 2
TN_SMALL = 512


def _params(sem, vmem_mib):
    return pltpu.CompilerParams(dimension_semantics=sem, vmem_limit_bytes=vmem_mib * MIB)


def _layernorm(z, g, b):
    mu = jnp.mean(z, axis=-1, keepdims=True)
    zc = z - mu
    var = jnp.mean(zc * zc, axis=-1, keepdims=True)
    return zc * lax.rsqrt(var + LN_EPS) * g + b


def _row_chunks(tm, n_chunks):
    rc = tm // n_chunks
    return [slice(r * rc, (r + 1) * rc) for r in range(n_chunks)]


def _store_norm(z, g_ref, b_ref, y_ref, yb_ref, rows):
    y = _layernorm(z, g_ref[...], b_ref[...])
    y_ref[rows, :] = y
    yb_ref[rows, :] = y.astype(BF16)


def _ffn_up_body(xb_ref, wg_ref, wu_ref, h_ref, *, n_chunks):
    for rows in _row_chunks(xb_ref.shape[0], n_chunks):
        xb = xb_ref[rows, :]
        gate = jnp.dot(xb, wg_ref[...], preferred_element_type=F32)
        up = jnp.dot(xb, wu_ref[...], preferred_element_type=F32)
        h_ref[rows, :] = (gate * jax.nn.sigmoid(gate) * up).astype(h_ref.dtype)


def _ffn_down_body(x_ref, h_ref, wd_ref, g_ref, b_ref, y_ref, yb_ref, *, n_chunks):
    for rows in _row_chunks(x_ref.shape[0], n_chunks):
        d = jnp.dot(h_ref[rows, :], wd_ref[...], preferred_element_type=F32)
        z = DEEPNORM_ALPHA * x_ref[rows, :] + 0.5 * d
        _store_norm(z, g_ref, b_ref, y_ref, yb_ref, rows)


def _ffn(x, xb, w_gu, w_down, ln_g, ln_b, layer, ln_idx, tm_up, ch_up, tm_down, ch_down):
    rows = x.shape[0]
    nj = D_FF // TF
    ln_row = layer * 3 + ln_idx
    hidden = pl.pallas_call(
        functools.partial(_ffn_up_body, n_chunks=ch_up),
        grid=(rows // tm_up, nj),
        in_specs=[
            pl.BlockSpec((tm_up, D_MODEL), lambda i, j: (i, 0)),
            pl.BlockSpec((None, D_MODEL, TF), lambda i, j: (layer, 0, j)),
            pl.BlockSpec((None, D_MODEL, TF), lambda i, j: (layer, 0, j + nj)),
        ],
        out_specs=pl.BlockSpec((tm_up, TF), lambda i, j: (i, j)),
        out_shape=jax.ShapeDtypeStruct((rows, D_FF), BF16),
        compiler_params=_params(("parallel", "arbitrary"), 48),
        name="ffn_up",
    )(xb, w_gu, w_gu)
    row_spec = pl.BlockSpec((tm_down, D_MODEL), lambda i: (i, 0))
    ln_spec = pl.BlockSpec((None, 1, D_MODEL), lambda i: (ln_row, 0, 0))
    return pl.pallas_call(
        functools.partial(_ffn_down_body, n_chunks=ch_down),
        grid=(rows // tm_down,),
        in_specs=[
            row_spec,
            pl.BlockSpec((tm_down, D_FF), lambda i: (i, 0)),
            pl.BlockSpec((None, D_FF, D_MODEL), lambda i: (layer, 0, 0), pipeline_mode=pl.Buffered(1)),
            ln_spec, ln_spec,
        ],
        out_specs=[row_spec, row_spec],
        out_shape=[jax.ShapeDtypeStruct((rows, D_MODEL), F32), jax.ShapeDtypeStruct((rows, D_MODEL), BF16)],
        compiler_params=_params(("parallel",), 52),
        name="ffn_down",
    )(x, hidden, w_down, ln_g, ln_b)


def _proj_body(xb_ref, w_ref, o_ref, *, n_chunks):
    for rows in _row_chunks(xb_ref.shape[0], n_chunks):
        o_ref[rows, :] = jnp.dot(xb_ref[rows, :], w_ref[...], preferred_element_type=F32).astype(o_ref.dtype)


def _proj(xb, w, layer, tn, col_block, n_col_blocks, tm, n_chunks, out_dtype):
    rows = xb.shape[0]
    return pl.pallas_call(
        functools.partial(_proj_body, n_chunks=n_chunks),
        grid=(rows // tm, n_col_blocks),
        in_specs=[
            pl.BlockSpec((tm, D_MODEL), lambda i, j: (i, 0)),
            pl.BlockSpec((None, D_MODEL, tn), lambda i, j: (layer, 0, col_block(j))),
        ],
        out_specs=pl.BlockSpec((tm, tn), lambda i, j: (i, j)),
        out_shape=jax.ShapeDtypeStruct((rows, n_col_blocks * tn), out_dtype),
        compiler_params=_params(("parallel", "arbitrary"), 52),
        name="proj",
    )(xb, w)


def _qk(q, k):
    return lax.dot_general(q, k, (((1,), (1,)), ((), ())), preferred_element_type=F32)


def _softmax_pv(scores, values, sink=None):
    m = scores[0].max(axis=-1, keepdims=True)
    for s in scores[1:]:
        m = jnp.maximum(m, s.max(axis=-1, keepdims=True))
    if sink is not None:
        m = jnp.maximum(m, sink)
    es = [jnp.exp(s - m) for s in scores]
    denom = es[0].sum(axis=-1, keepdims=True)
    for e in es[1:]:
        denom = denom + e.sum(axis=-1, keepdims=True)
    if sink is not None:
        denom = denom + jnp.exp(sink - m)
    o = jnp.dot(es[0].astype(BF16), values[0], preferred_element_type=F32)
    for e, v in zip(es[1:], values[1:]):
        o = o + jnp.dot(e.astype(BF16), v, preferred_element_type=F32)
    return o / denom


def _hs(h, width=HEAD_DIM):
    return slice(h * width, (h + 1) * width)


def _attn_prompt_body(sink_ref, cur_ref, ka2_ref, ka1_ref, va2_ref, va1_ref, kbp_ref, vbp_ref,
                      mk_ref, mv_ref, bias_a_ref, bias_b_ref, o_ref):
    i = pl.program_id(1)
    scale = F32(HEAD_DIM ** -0.5)
    rep = B_HEADS // B_KV_HEADS
    half = QB // 2

    def cur(c0, h, width=HEAD_DIM):
        return cur_ref[:, c0 + h * width:c0 + (h + 1) * width]

    pen2 = jnp.where(i >= 2, F32(0.0), F32(NEG_INF))
    pen1 = jnp.where(i >= 1, F32(0.0), F32(NEG_INF))

    for h in range(A_HEADS):
        q = cur(COL_QA, h)
        s2 = _qk(q, ka2_ref[:, _hs(h)]) * scale + bias_a_ref[h, :, 0:QB] + pen2
        s1 = _qk(q, ka1_ref[:, _hs(h)]) * scale + bias_a_ref[h, :, QB:2 * QB] + pen1
        s0 = _qk(q, cur(COL_KA, h)) * scale + bias_a_ref[h, :, 2 * QB:3 * QB]
        vs = [va2_ref[:, _hs(h)], va1_ref[:, _hs(h)], cur(COL_VA, h)]
        o_ref[:, _hs(h)] = _softmax_pv([s2, s1, s0], vs).astype(o_ref.dtype)

    for h in range(B_HEADS):
        g = h // rep
        q = cur(COL_QB, h)
        s1 = _qk(q, kbp_ref[:, _hs(g)]) * scale + bias_b_ref[h, :, 0:half] + pen1
        s0 = _qk(q, cur(COL_KB, g)) * scale + bias_b_ref[h, :, half:half + QB]
        o = _softmax_pv([s1, s0], [vbp_ref[:, _hs(g)], cur(COL_VB, g)], sink=sink_ref[h])
        o_ref[:, A_WIDTH + h * HEAD_DIM:A_WIDTH + (h + 1) * HEAD_DIM] = o.astype(o_ref.dtype)

    mscale = F32(M_HEAD_DIM ** -0.5)
    for h in range(M_HEADS):
        s = _qk(cur(COL_QM, h, M_HEAD_DIM), mk_ref[:, _hs(h, M_HEAD_DIM)]) * mscale
        o = _softmax_pv([s], [mv_ref[:, _hs(h, M_HEAD_DIM)]])
        base = A_WIDTH + B_Q_WIDTH
        o_ref[:, base + h * M_HEAD_DIM:base + (h + 1) * M_HEAD_DIM] = o.astype(o_ref.dtype)


def _attn_prompt(p, mk, mv, bias_a, bias_b, sink, batch, seq):
    nb = seq // QB
    half = QB // 2
    ka_blk, va_blk = COL_KA // A_WIDTH, COL_VA // A_WIDTH
    kb_blk, vb_blk = COL_KB // B_KV_WIDTH, COL_VB // B_KV_WIDTH

    def prev_a(col, back):
        return pl.BlockSpec((QB, A_WIDTH), lambda b, i: (b * nb + jnp.maximum(i - back, 0), col))

    def prev_b(col):
        return pl.BlockSpec((half, B_KV_WIDTH),
                            lambda b, i: (b * 2 * nb + jnp.maximum(2 * i - 1, 0), col))

    return pl.pallas_call(
        _attn_prompt_body,
        grid=(batch, nb),
        in_specs=[
            pl.BlockSpec(memory_space=pltpu.SMEM),
            pl.BlockSpec((QB, QKV_WIDTH), lambda b, i: (b * nb + i, 0)),
            prev_a(ka_blk, 2), prev_a(ka_blk, 1), prev_a(va_blk, 2), prev_a(va_blk, 1),
            prev_b(kb_blk), prev_b(vb_blk),
            pl.BlockSpec((N_MEM, M_WIDTH), lambda b, i: (b, 0)),
            pl.BlockSpec((N_MEM, M_WIDTH), lambda b, i: (b, 0)),
            pl.BlockSpec((A_HEADS, QB, 3 * QB), lambda b, i: (0, 0, 0)),
            pl.BlockSpec((B_HEADS, QB, half + QB), lambda b, i: (0, 0, 0)),
        ],
        out_specs=pl.BlockSpec((QB, ATT_WIDTH), lambda b, i: (b * nb + i, 0)),
        out_shape=jax.ShapeDtypeStruct((batch * seq, ATT_WIDTH), BF16),
        compiler_params=_params(("parallel", "arbitrary"), 48),
        name="attn_prompt",
    )(sink, p, p, p, p, p, p, p, mk, mv, bias_a, bias_b)


def _attn_sample_body(sink_ref, cur_ref, cak_ref, cav_ref, cbk_ref, cbv_ref, cmk_ref, cmv_ref,
                      bias_ac_ref, bias_an_ref, bias_bc_ref, bias_bn_ref, o_ref):
    scale = F32(HEAD_DIM ** -0.5)

    def col(c0, h, width=HEAD_DIM):
        return cur_ref[:, c0 + h * width:c0 + (h + 1) * width].astype(BF16)

    for h in range(A_HEADS):
        q = col(COL_QA, h)
        ks = [cak_ref[:, _hs(h)].astype(BF16), col(COL_KA, h)]
        vs = [cav_ref[:, _hs(h)].astype(BF16), col(COL_VA, h)]
        sc = _qk(q, ks[0]) * scale + bias_ac_ref[h]
        sn = _qk(q, ks[1]) * scale + bias_an_ref[h]
        o_ref[:, _hs(h)] = _softmax_pv([sc, sn], vs).astype(o_ref.dtype)

    rep = B_HEADS // B_KV_HEADS
    for h in range(B_HEADS):
        g = h // rep
        q = col(COL_QB, h)
        ks = [cbk_ref[:, _hs(g)].astype(BF16), col(COL_KB, g)]
        vs = [cbv_ref[:, _hs(g)].astype(BF16), col(COL_VB, g)]
        sc = _qk(q, ks[0]) * scale + bias_bc_ref[h]
        sn = _qk(q, ks[1]) * scale + bias_bn_ref[h]
        o = _softmax_pv([sc, sn], vs, sink=sink_ref[h])
        o_ref[:, A_WIDTH + h * HEAD_DIM:A_WIDTH + (h + 1) * HEAD_DIM] = o.astype(o_ref.dtype)

    mscale = F32(M_HEAD_DIM ** -0.5)
    for h in range(M_HEADS):
        q = col(COL_QM, h, M_HEAD_DIM)
        s = _qk(q, cmk_ref[:, _hs(h, M_HEAD_DIM)].astype(BF16)) * mscale
        o = _softmax_pv([s], [cmv_ref[:, _hs(h, M_HEAD_DIM)].astype(BF16)])
        base = A_WIDTH + B_Q_WIDTH
        o_ref[:, base + h * M_HEAD_DIM:base + (h + 1) * M_HEAD_DIM] = o.astype(o_ref.dtype)


def _attn_sample(p, cak, cav, cbk, cbv, cmk, cmv, biases, sink, layer, streams, t):
    bias_ac, bias_an, bias_bc, bias_bn = biases

    def cache_spec(c):
        return pl.BlockSpec((None, None, c.shape[2], c.shape[3]), lambda s: (layer, s, 0, 0))

    def full_spec(a):
        return pl.BlockSpec(a.shape, lambda s: (0, 0, 0))

    return pl.pallas_call(
        _attn_sample_body,
        grid=(streams,),
        in_specs=[
            pl.BlockSpec(memory_space=pltpu.SMEM),
            pl.BlockSpec((t, QKV_WIDTH), lambda s: (s, 0)),
            cache_spec(cak), cache_spec(cav), cache_spec(cbk), cache_spec(cbv),
            cache_spec(cmk), cache_spec(cmv),
            full_spec(bias_ac), full_spec(bias_an), full_spec(bias_bc), full_spec(bias_bn),
        ],
        out_specs=pl.BlockSpec((t, ATT_WIDTH), lambda s: (s, 0)),
        out_shape=jax.ShapeDtypeStruct((streams * t, ATT_WIDTH), BF16),
        compiler_params=_params(("arbitrary",), 32),
        name="attn_sample",
    )(sink, p, cak, cav, cbk, cbv, cmk, cmv, bias_ac, bias_an, bias_bc, bias_bn)


def _merge_body(xb_ref, o_ref, wg_ref, wbr_ref, h_ref, acc_ref, *, n_branch, n_chunks):
    j = pl.program_id(1)

    def step(first, last):
        for rows in _row_chunks(xb_ref.shape[0], n_chunks):
            gate = jax.nn.sigmoid(jnp.dot(xb_ref[rows, :], wg_ref[...], preferred_element_type=F32))
            term = gate * jnp.dot(o_ref[rows, :], wbr_ref[...], preferred_element_type=F32)
            if last:
                h_ref[rows, :] = (acc_ref[rows, :] + term).astype(h_ref.dtype)
            elif first:
                acc_ref[rows, :] = term
            else:
                acc_ref[rows, :] += term

    pl.when(j == 0)(lambda: step(True, False))
    pl.when(jnp.logical_and(j > 0, j < n_branch - 1))(lambda: step(False, False))
    pl.when(j == n_branch - 1)(lambda: step(False, True))


def _merge(xb, att, w_gate, w_br, layer, tm, n_chunks):
    rows = xb.shape[0]
    n_branch = 3
    return pl.pallas_call(
        functools.partial(_merge_body, n_branch=n_branch, n_chunks=n_chunks),
        grid=(rows // tm, n_branch),
        in_specs=[
            pl.BlockSpec((tm, D_MODEL), lambda i, j: (i, 0)),
            pl.BlockSpec((tm, A_WIDTH), lambda i, j: (i, j)),
            pl.BlockSpec((None, D_MODEL, D_MODEL), lambda i, j: (layer, 0, j)),
            pl.BlockSpec((None, None, A_WIDTH, D_MODEL), lambda i, j: (layer, j, 0, 0)),
        ],
        out_specs=pl.BlockSpec((tm, D_MODEL), lambda i, j: (i, 0)),
        out_shape=jax.ShapeDtypeStruct((rows, D_MODEL), BF16),
        scratch_shapes=[pltpu.VMEM((tm, D_MODEL), F32)],
        compiler_params=_params(("parallel", "arbitrary"), 52),
        name="merge",
    )(xb, att, w_gate, w_br)


def _outproj_body(x_ref, h_ref, w_ref, g_ref, b_ref, y_ref, yb_ref, *, n_chunks):
    for rows in _row_chunks(x_ref.shape[0], n_chunks):
        mix = jnp.dot(h_ref[rows, :], w_ref[...], preferred_element_type=F32)
        z = DEEPNORM_ALPHA * x_ref[rows, :] + mix
        _store_norm(z, g_ref, b_ref, y_ref, yb_ref, rows)


def _outproj(x, h, w_out, ln_g, ln_b, layer, tm, n_chunks):
    rows = x.shape[0]
    ln_row = layer * 3 + 1
    row_spec = pl.BlockSpec((tm, D_MODEL), lambda i: (i, 0))
    return pl.pallas_call(
        functools.partial(_outproj_body, n_chunks=n_chunks),
        grid=(rows // tm,),
        in_specs=[
            row_spec, row_spec,
            pl.BlockSpec((None, D_MODEL, D_MODEL), lambda i: (layer, 0, 0)),
            pl.BlockSpec((None, 1, D_MODEL), lambda i: (ln_row, 0, 0)),
            pl.BlockSpec((None, 1, D_MODEL), lambda i: (ln_row, 0, 0)),
        ],
        out_specs=[row_spec, row_spec],
        out_shape=[jax.ShapeDtypeStruct((rows, D_MODEL), F32), jax.ShapeDtypeStruct((rows, D_MODEL), BF16)],
        compiler_params=_params(("parallel",), 52),
        name="outproj",
    )(x, h, w_out, ln_g, ln_b)


def _toeplitz(row_of_delta, n_q, n_k):
    period = n_q + n_k
    k = np.arange(period)
    w = row_of_delta(np.where(k < n_k, k, k - period))
    return jnp.tile(w, (1, n_q))[:, :n_q * (period - 1)].reshape(-1, n_q, period - 1)[:, :, :n_k]


def _rel_table_bias(table, offset, n_q, n_k):
    def row(delta):
        idx = np.clip(offset - delta, -(CHUNK - 1), A_REL_MAX) + (CHUNK - 1)
        return table.astype(F32)[:, idx]
    return _toeplitz(row, n_q, n_k)


def _alibi(offset, n_q, n_k):
    slopes = 2.0 ** (-8.0 * (jnp.arange(B_HEADS, dtype=F32) + 1.0) / B_HEADS)
    rel = offset + np.arange(n_q)[:, None] - np.arange(n_k)[None, :]
    return -slopes[:, None, None] * jnp.asarray(np.abs(rel), dtype=F32)


def _band(n_keys, n_prev):
    cq = np.arange(QB)[:, None] // CHUNK
    ck = np.arange(n_keys)[None, :] // CHUNK
    return (ck >= cq) & (ck <= cq + n_prev)


def _prompt_biases(table):
    bias_a = jnp.where(_band(3 * QB, A_PREV_CHUNKS)[None],
                       _rel_table_bias(table, A_PREV_CHUNKS * CHUNK, QB, 3 * QB), NEG_INF)
    nkb = QB // 2 + QB
    bias_b = jnp.where(_band(nkb, B_PREV_CHUNKS)[None], _alibi(B_PREV_CHUNKS * CHUNK, QB, nkb), NEG_INF)
    return bias_a, bias_b


def _sample_biases(table, wa, wb, t):
    ba = _rel_table_bias(table, wa, t, wa + t)
    bb = _alibi(wb, t, wb + t)
    return ba[:, :, :wa], ba[:, :, wa:], bb[:, :, :wb], bb[:, :, wb:]


def kernel(x_prompt, x_sample, cache_a_k, cache_a_v, cache_b_k, cache_b_v, cache_mem_k, cache_mem_v,
           mem_prompt, w_in, w_br_a, w_br_b, w_br_m, w_out, w_mem_kv, rel_bias_a, sink_b,
           ffn1_gu, ffn1_down, ffn2_gu, ffn2_down, ln_g, ln_b):
    batch, seq, _ = x_prompt.shape
    streams, t, _ = x_sample.shape
    a_keep = min(A_PREV_CHUNKS * CHUNK, seq)
    b_keep = min(B_PREV_CHUNKS * CHUNK, seq)
    wa, wb = cache_a_k.shape[2], cache_b_k.shape[2]

    w_qkv = w_in[:, :, :QKV_WIDTH].astype(BF16)
    w_gate = w_in[:, :, QKV_WIDTH:].astype(BF16)
    w_br = jnp.stack([w_br_a, w_br_b, w_br_m], axis=1).astype(BF16)
    w_out_b = w_out.astype(BF16)
    w_mem_b = w_mem_kv.astype(BF16)
    f1gu, f1d = ffn1_gu.astype(BF16), ffn1_down.astype(BF16)
    f2gu, f2d = ffn2_gu.astype(BF16), ffn2_down.astype(BF16)
    lng = ln_g.reshape(DEPTH * 3, 1, D_MODEL)
    lnb = ln_b.reshape(DEPTH * 3, 1, D_MODEL)

    cak = cache_a_k.reshape(DEPTH, streams, wa, A_WIDTH)
    cav = cache_a_v.reshape(DEPTH, streams, wa, A_WIDTH)
    cbk = cache_b_k.reshape(DEPTH, streams, wb, B_KV_WIDTH)
    cbv = cache_b_v.reshape(DEPTH, streams, wb, B_KV_WIDTH)
    cmk = cache_mem_k.reshape(DEPTH, streams, N_MEM, M_WIDTH)
    cmv = cache_mem_v.reshape(DEPTH, streams, N_MEM, M_WIDTH)

    xp = x_prompt.reshape(batch * seq, D_MODEL)
    xs = x_sample.reshape(streams * t, D_MODEL)
    xpb, xsb = xp.astype(BF16), xs.astype(BF16)
    memb = mem_prompt.reshape(batch * N_MEM, D_MODEL).astype(BF16)

    tm_p, ch_p = 512, 2
    tm_s = streams * t
    ffn_tiles_p = (2048, 4, 256, 2)
    ffn_tiles_s = (tm_s, 1, tm_s, 1)
    n_small_qkv = QKV_WIDTH // TN_SMALL
    n_small_mem = 2 * M_WIDTH // TN_SMALL
    kv_first, kv_b = COL_KA // TN_SMALL, COL_KB // TN_SMALL
    n_kv_a = 2 * A_WIDTH // TN_SMALL

    akp, avp, bkp, bvp, mkp, mvp = [], [], [], [], [], []
    aks, avs, bks, bvs = [], [], [], []
    for l in range(DEPTH):
        bias_a, bias_b = _prompt_biases(rel_bias_a[l])
        s_biases = _sample_biases(rel_bias_a[l], wa, wb, t)
        sink = sink_b[l].astype(F32)

        xp, xpb = _ffn(xp, xpb, f1gu, f1d, lng, lnb, l, 0, *ffn_tiles_p)
        p = _proj(xpb, w_qkv, l, TN_QKV, lambda j: j, QKV_WIDTH // TN_QKV, 1024, 4, BF16)
        tail = xpb.reshape(batch, seq, D_MODEL)[:, seq - a_keep:].reshape(batch * a_keep, D_MODEL)
        kv = _proj(tail, w_qkv, l, TN_SMALL, lambda j: jnp.where(j < n_kv_a, j + kv_first, kv_b),
                   n_kv_a + 1, 512, 1, F32).reshape(batch, a_keep, -1)
        mkv = _proj(memb, w_mem_b, l, TN_SMALL, lambda j: j, n_small_mem, batch * N_MEM, 1, F32)
        mk, mv = mkv[:, :M_WIDTH], mkv[:, M_WIDTH:]
        att = _attn_prompt(p, mk.astype(BF16), mv.astype(BF16), bias_a, bias_b, sink, batch, seq)
        h = _merge(xpb, att, w_gate, w_br, l, tm_p, ch_p)
        xp, xpb = _outproj(xp, h, w_out_b, lng, lnb, l, tm_p, ch_p)
        xp, xpb = _ffn(xp, xpb, f2gu, f2d, lng, lnb, l, 2, *ffn_tiles_p)

        akp.append(kv[:, :, :A_WIDTH].reshape(batch, a_keep, A_HEADS, HEAD_DIM))
        avp.append(kv[:, :, A_WIDTH:2 * A_WIDTH].reshape(batch, a_keep, A_HEADS, HEAD_DIM))
        kvb = kv[:, a_keep - b_keep:, 2 * A_WIDTH:]
        bkp.append(kvb[:, :, :B_KV_WIDTH].reshape(batch, b_keep, B_KV_HEADS, HEAD_DIM))
        bvp.append(kvb[:, :, B_KV_WIDTH:].reshape(batch, b_keep, B_KV_HEADS, HEAD_DIM))
        mkp.append(mk.reshape(batch, N_MEM, M_HEADS, M_HEAD_DIM))
        mvp.append(mv.reshape(batch, N_MEM, M_HEADS, M_HEAD_DIM))

        xs, xsb = _ffn(xs, xsb, f1gu, f1d, lng, lnb, l, 0, *ffn_tiles_s)
        ps = _proj(xsb, w_qkv, l, TN_SMALL, lambda j: j, n_small_qkv, tm_s, 1, F32)
        att_s = _attn_sample(ps, cak, cav, cbk, cbv, cmk, cmv, s_biases, sink, l, streams, t)
        hs = _merge(xsb, att_s, w_gate, w_br, l, tm_s, 1)
        xs, xsb = _outproj(xs, hs, w_out_b, lng, lnb, l, tm_s, 1)
        xs, xsb = _ffn(xs, xsb, f2gu, f2d, lng, lnb, l, 2, *ffn_tiles_s)

        ps3 = ps.reshape(streams, t, QKV_WIDTH)
        aks.append(ps3[:, :, COL_KA:COL_KA + A_WIDTH].reshape(streams, t, A_HEADS, HEAD_DIM))
        avs.append(ps3[:, :, COL_VA:COL_VA + A_WIDTH].reshape(streams, t, A_HEADS, HEAD_DIM))
        bks.append(ps3[:, :, COL_KB:COL_KB + B_KV_WIDTH].reshape(streams, t, B_KV_HEADS, HEAD_DIM))
        bvs.append(ps3[:, :, COL_VB:COL_VB + B_KV_WIDTH].reshape(streams, t, B_KV_HEADS, HEAD_DIM))

    return (xp.reshape(batch, seq, D_MODEL), xs.reshape(streams, t, D_MODEL),
            jnp.stack(akp), jnp.stack(avp), jnp.stack(bkp), jnp.stack(bvp), jnp.stack(mkp), jnp.stack(mvp),
            jnp.stack(aks), jnp.stack(avs), jnp.stack(bks), jnp.stack(bvs))
```

```python
import functools

import numpy as np
import jax
import jax.numpy as jnp
from jax import lax
from jax.experimental import pallas as pl
from jax.experimental.pallas import tpu as pltpu

F32 = jnp.float32
BF16 = jnp.bfloat16

D_MODEL = 2048
DEPTH = 4
CHUNK = 64
HEAD_DIM = 128
A_HEADS = 8
A_PREV_CHUNKS = 8
A_REL_MAX = 128
B_HEADS = 8
B_KV_HEADS = 2
B_PREV_CHUNKS = 2
M_HEADS = 4
M_HEAD_DIM = 256
N_MEM = 256
D_FF = 5632

A_WIDTH = A_HEADS * HEAD_DIM
B_Q_WIDTH = B_HEADS * HEAD_DIM
B_KV_WIDTH = B_KV_HEADS * HEAD_DIM
M_WIDTH = M_HEADS * M_HEAD_DIM
QKV_WIDTH = 3 * A_WIDTH + B_Q_WIDTH + 2 * B_KV_WIDTH + M_WIDTH
COL_QA, COL_KA, COL_VA = 0, A_WIDTH, 2 * A_WIDTH
COL_QB = 3 * A_WIDTH
COL_KB = COL_QB + B_Q_WIDTH
COL_VB = COL_KB + B_KV_WIDTH
COL_QM = COL_VB + B_KV_WIDTH
ATT_WIDTH = A_WIDTH + B_Q_WIDTH + M_WIDTH

DEEPNORM_ALPHA = (2 * DEPTH) ** 0.25
LN_EPS = 1e-5
NEG_INF = -1e30

MIB = 1024 * 1024
QB = 4 * CHUNK
N_BIAS_VARIANTS = A_PREV_CHUNKS * CHUNK // QB + 1
HEAD_GROUP = 4
TF = 512
TN_QKV = QKV_WIDTH // 2
TN_SMALL = 512


def _params(sem, vmem_mib):
    return pltpu.CompilerParams(dimension_semantics=sem, vmem_limit_bytes=vmem_mib * MIB)


def _layernorm(z, g, b):
    mu = jnp.mean(z, axis=-1, keepdims=True)
    zc = z - mu
    var = jnp.mean(zc * zc, axis=-1, keepdims=True)
    return zc * lax.rsqrt(var + LN_EPS) * g + b


def _row_chunks(tm, n_chunks):
    rc = tm // n_chunks
    return [slice(r * rc, (r + 1) * rc) for r in range(n_chunks)]


def _store_norm(z, g_ref, b_ref, y_ref, yb_ref, rows):
    y = _layernorm(z, g_ref[...], b_ref[...])
    y_ref[rows, :] = y
    yb_ref[rows, :] = y.astype(BF16)


def _ffn_up_body(xb_ref, wg_ref, wu_ref, h_ref, *, n_chunks):
    for rows in _row_chunks(xb_ref.shape[0], n_chunks):
        xb = xb_ref[rows, :]
        gate = jnp.dot(xb, wg_ref[...], preferred_element_type=F32)
        up = jnp.dot(xb, wu_ref[...], preferred_element_type=F32)
        h_ref[rows, :] = (gate * jax.nn.sigmoid(gate) * up).astype(h_ref.dtype)


def _ffn_down_body(x_ref, h_ref, wd_ref, g_ref, b_ref, y_ref, yb_ref, *, n_chunks):
    for rows in _row_chunks(x_ref.shape[0], n_chunks):
        d = jnp.dot(h_ref[rows, :], wd_ref[...], preferred_element_type=F32)
        z = DEEPNORM_ALPHA * x_ref[rows, :] + 0.5 * d
        _store_norm(z, g_ref, b_ref, y_ref, yb_ref, rows)


def _ffn(x, xb, w_gu, w_down, ln_g, ln_b, layer, ln_idx, tm_up, ch_up, tm_down, ch_down):
    rows = x.shape[0]
    nj = D_FF // TF
    ln_row = layer * 3 + ln_idx
    hidden = pl.pallas_call(
        functools.partial(_ffn_up_body, n_chunks=ch_up),
        grid=(rows // tm_up, nj),
        in_specs=[
            pl.BlockSpec((tm_up, D_MODEL), lambda i, j: (i, 0)),
            pl.BlockSpec((None, D_MODEL, TF), lambda i, j: (layer, 0, j)),
            pl.BlockSpec((None, D_MODEL, TF), lambda i, j: (layer, 0, j + nj)),
        ],
        out_specs=pl.BlockSpec((tm_up, TF), lambda i, j: (i, j)),
        out_shape=jax.ShapeDtypeStruct((rows, D_FF), BF16),
        compiler_params=_params(("parallel", "arbitrary"), 48),
        name="ffn_up",
    )(xb, w_gu, w_gu)
    row_spec = pl.BlockSpec((tm_down, D_MODEL), lambda i: (i, 0))
    ln_spec = pl.BlockSpec((None, 1, D_MODEL), lambda i: (ln_row, 0, 0))
    return pl.pallas_call(
        functools.partial(_ffn_down_body, n_chunks=ch_down),
        grid=(rows // tm_down,),
        in_specs=[
            row_spec,
            pl.BlockSpec((tm_down, D_FF), lambda i: (i, 0)),
            pl.BlockSpec((None, D_FF, D_MODEL), lambda i: (layer, 0, 0), pipeline_mode=pl.Buffered(1)),
            ln_spec, ln_spec,
        ],
        out_specs=[row_spec, row_spec],
        out_shape=[jax.ShapeDtypeStruct((rows, D_MODEL), F32), jax.ShapeDtypeStruct((rows, D_MODEL), BF16)],
        compiler_params=_params(("parallel",), 52),
        name="ffn_down",
    )(x, hidden, w_down, ln_g, ln_b)


def _proj_body(xb_ref, w_ref, o_ref, *, n_chunks):
    for rows in _row_chunks(xb_ref.shape[0], n_chunks):
        o_ref[rows, :] = jnp.dot(xb_ref[rows, :], w_ref[...], preferred_element_type=F32).astype(o_ref.dtype)


def _proj(xb, w, layer, tn, col_block, n_col_blocks, tm, n_chunks, out_dtype):
    rows = xb.shape[0]
    return pl.pallas_call(
        functools.partial(_proj_body, n_chunks=n_chunks),
        grid=(rows // tm, n_col_blocks),
        in_specs=[
            pl.BlockSpec((tm, D_MODEL), lambda i, j: (i, 0)),
            pl.BlockSpec((None, D_MODEL, tn), lambda i, j: (layer, 0, col_block(j))),
        ],
        out_specs=pl.BlockSpec((tm, tn), lambda i, j: (i, j)),
        out_shape=jax.ShapeDtypeStruct((rows, n_col_blocks * tn), out_dtype),
        compiler_params=_params(("parallel", "arbitrary"), 52),
        name="proj",
    )(xb, w)


def _qk(q, k):
    return lax.dot_general(q, k, (((1,), (1,)), ((), ())), preferred_element_type=F32)


def _softmax_pv(scores, values, scale, sink=None):
    return _softmax_pv_group([(scores, values, sink)], scale)[0]


def _softmax_pv_group(heads, scale):
    c = F32(scale * np.log2(np.e))
    ms = []
    for scores, _, sink in heads:
        m = scores[0].max(axis=-1, keepdims=True)
        for s in scores[1:]:
            m = jnp.maximum(m, s.max(axis=-1, keepdims=True))
        ms.append(m if sink is None else jnp.maximum(m, sink))
    es = [[jnp.exp2((s - m) * c) for s in scores] for (scores, _, _), m in zip(heads, ms)]
    denoms = []
    for (_, _, sink), m, e_blocks in zip(heads, ms, es):
        denom = e_blocks[0].sum(axis=-1, keepdims=True)
        for e in e_blocks[1:]:
            denom = denom + e.sum(axis=-1, keepdims=True)
        denoms.append(denom if sink is None else denom + jnp.exp2((sink - m) * c))
    outs = []
    for (_, values, _), e_blocks, denom in zip(heads, es, denoms):
        o = jnp.dot(e_blocks[0].astype(BF16), values[0], preferred_element_type=F32)
        for e, v in zip(e_blocks[1:], values[1:]):
            o = o + jnp.dot(e.astype(BF16), v, preferred_element_type=F32)
        outs.append(o / denom)
    return outs


def _hs(h, width=HEAD_DIM):
    return slice(h * width, (h + 1) * width)


def _attn_prompt_body(sink_ref, cur_ref, ka2_ref, ka1_ref, va2_ref, va1_ref, kbp_ref, vbp_ref,
                      mk_ref, mv_ref, bias_a_ref, bias_b_ref, o_ref):
    scale = HEAD_DIM ** -0.5
    rep = B_HEADS // B_KV_HEADS
    half = QB // 2

    def cur(c0, h, width=HEAD_DIM):
        return cur_ref[:, c0 + h * width:c0 + (h + 1) * width]

    def head_a(h):
        q = cur(COL_QA, h)
        t2 = _qk(q, ka2_ref[:, _hs(h)]) + bias_a_ref[h, :, 0:QB]
        t1 = _qk(q, ka1_ref[:, _hs(h)]) + bias_a_ref[h, :, QB:2 * QB]
        t0 = _qk(q, cur(COL_KA, h)) + bias_a_ref[h, :, 2 * QB:3 * QB]
        return [t2, t1, t0], [va2_ref[:, _hs(h)], va1_ref[:, _hs(h)], cur(COL_VA, h)], None

    def head_b(h):
        g = h // rep
        q = cur(COL_QB, h)
        t1 = _qk(q, kbp_ref[:, _hs(g)]) + bias_b_ref[h, :, 0:half]
        t0 = _qk(q, cur(COL_KB, g)) + bias_b_ref[h, :, half:half + QB]
        return [t1, t0], [vbp_ref[:, _hs(g)], cur(COL_VB, g)], sink_ref[h]

    def head_m(h):
        t = _qk(cur(COL_QM, h, M_HEAD_DIM), mk_ref[:, _hs(h, M_HEAD_DIM)])
        return [t], [mv_ref[:, _hs(h, M_HEAD_DIM)]], None

    branches = ((head_a, A_HEADS, HEAD_DIM, scale, 0), (head_b, B_HEADS, HEAD_DIM, scale, A_WIDTH),
                (head_m, M_HEADS, M_HEAD_DIM, M_HEAD_DIM ** -0.5, A_WIDTH + B_Q_WIDTH))
    for head, n_heads, width, sc, base in branches:
        for h0 in range(0, n_heads, HEAD_GROUP):
            outs = _softmax_pv_group([head(h) for h in range(h0, h0 + HEAD_GROUP)], sc)
            for h, o in zip(range(h0, h0 + HEAD_GROUP), outs):
                o_ref[:, base + h * width:base + (h + 1) * width] = o.astype(o_ref.dtype)


def _attn_prompt(p, mk, mv, bias_a, bias_b, sink, batch, seq):
    nb = seq // QB
    half = QB // 2
    ka_blk, va_blk = COL_KA // A_WIDTH, COL_VA // A_WIDTH
    kb_blk, vb_blk = COL_KB // B_KV_WIDTH, COL_VB // B_KV_WIDTH

    def prev_a(col, back):
        return pl.BlockSpec((QB, A_WIDTH), lambda b, i: (b * nb + jnp.maximum(i - back, 0), col))

    def prev_b(col):
        return pl.BlockSpec((half, B_KV_WIDTH),
                            lambda b, i: (b * 2 * nb + jnp.maximum(2 * i - 1, 0), col))

    return pl.pallas_call(
        _attn_prompt_body,
        grid=(batch, nb),
        in_specs=[
            pl.BlockSpec(memory_space=pltpu.SMEM),
            pl.BlockSpec((QB, QKV_WIDTH), lambda b, i: (b * nb + i, 0)),
            prev_a(ka_blk, 2), prev_a(ka_blk, 1), prev_a(va_blk, 2), prev_a(va_blk, 1),
            prev_b(kb_blk), prev_b(vb_blk),
            pl.BlockSpec((N_MEM, M_WIDTH), lambda b, i: (b, 0)),
            pl.BlockSpec((N_MEM, M_WIDTH), lambda b, i: (b, 0)),
            pl.BlockSpec((None, A_HEADS, QB, 3 * QB), lambda b, i: (jnp.minimum(i, N_BIAS_VARIANTS - 1), 0, 0, 0)),
            pl.BlockSpec((None, B_HEADS, QB, half + QB),
                         lambda b, i: (jnp.minimum(i, N_BIAS_VARIANTS - 1), 0, 0, 0)),
        ],
        out_specs=pl.BlockSpec((QB, ATT_WIDTH), lambda b, i: (b * nb + i, 0)),
        out_shape=jax.ShapeDtypeStruct((batch * seq, ATT_WIDTH), BF16),
        compiler_params=_params(("parallel", "arbitrary"), 48),
        name="attn_prompt",
    )(sink, p, p, p, p, p, p, p, mk, mv, bias_a, bias_b)


def _attn_sample_body(sink_ref, cur_ref, cak_ref, cav_ref, cbk_ref, cbv_ref, cmk_ref, cmv_ref,
                      bias_ac_ref, bias_an_ref, bias_bc_ref, bias_bn_ref, o_ref):
    scale = HEAD_DIM ** -0.5

    def col(c0, h, width=HEAD_DIM):
        return cur_ref[:, c0 + h * width:c0 + (h + 1) * width].astype(BF16)

    for h in range(A_HEADS):
        q = col(COL_QA, h)
        ks = [cak_ref[:, h, :].astype(BF16), col(COL_KA, h)]
        vs = [cav_ref[:, h, :].astype(BF16), col(COL_VA, h)]
        tc = _qk(q, ks[0]) + bias_ac_ref[h]
        tn = _qk(q, ks[1]) + bias_an_ref[h]
        o_ref[:, _hs(h)] = _softmax_pv([tc, tn], vs, scale).astype(o_ref.dtype)

    rep = B_HEADS // B_KV_HEADS
    for h in range(B_HEADS):
        g = h // rep
        q = col(COL_QB, h)
        ks = [cbk_ref[:, _hs(g)].astype(BF16), col(COL_KB, g)]
        vs = [cbv_ref[:, _hs(g)].astype(BF16), col(COL_VB, g)]
        tc = _qk(q, ks[0]) + bias_bc_ref[h]
        tn = _qk(q, ks[1]) + bias_bn_ref[h]
        o = _softmax_pv([tc, tn], vs, scale, sink=sink_ref[h])
        o_ref[:, A_WIDTH + h * HEAD_DIM:A_WIDTH + (h + 1) * HEAD_DIM] = o.astype(o_ref.dtype)

    for h in range(M_HEADS):
        q = col(COL_QM, h, M_HEAD_DIM)
        t = _qk(q, cmk_ref[:, _hs(h, M_HEAD_DIM)].astype(BF16))
        o = _softmax_pv([t], [cmv_ref[:, _hs(h, M_HEAD_DIM)].astype(BF16)], M_HEAD_DIM ** -0.5)
        base = A_WIDTH + B_Q_WIDTH
        o_ref[:, base + h * M_HEAD_DIM:base + (h + 1) * M_HEAD_DIM] = o.astype(o_ref.dtype)


def _attn_sample(p, cak, cav, cbk, cbv, cmk, cmv, biases, sink, layer, streams, t):
    bias_ac, bias_an, bias_bc, bias_bn = biases

    def cache_spec(c):
        return pl.BlockSpec((None, None) + c.shape[2:], lambda s: (layer, s) + (0,) * (c.ndim - 2))

    def full_spec(a):
        return pl.BlockSpec(a.shape, lambda s: (0, 0, 0))

    return pl.pallas_call(
        _attn_sample_body,
        grid=(streams,),
        in_specs=[
            pl.BlockSpec(memory_space=pltpu.SMEM),
            pl.BlockSpec((t, QKV_WIDTH), lambda s: (s, 0)),
            cache_spec(cak), cache_spec(cav), cache_spec(cbk), cache_spec(cbv),
            cache_spec(cmk), cache_spec(cmv),
            full_spec(bias_ac), full_spec(bias_an), full_spec(bias_bc), full_spec(bias_bn),
        ],
        out_specs=pl.BlockSpec((t, ATT_WIDTH), lambda s: (s, 0)),
        out_shape=jax.ShapeDtypeStruct((streams * t, ATT_WIDTH), BF16),
        compiler_params=_params(("arbitrary",), 32),
        name="attn_sample",
    )(sink, p, cak, cav, cbk, cbv, cmk, cmv, bias_ac, bias_an, bias_bc, bias_bn)


def _merge_body(xb_ref, o_ref, wg_ref, wbr_ref, h_ref, acc_ref, *, n_branch, n_chunks):
    j = pl.program_id(1)

    def step(first, last):
        for rows in _row_chunks(xb_ref.shape[0], n_chunks):
            gate = jax.nn.sigmoid(jnp.dot(xb_ref[rows, :], wg_ref[...], preferred_element_type=F32))
            term = gate * jnp.dot(o_ref[rows, :], wbr_ref[...], preferred_element_type=F32)
            if last:
                h_ref[rows, :] = (acc_ref[rows, :] + term).astype(h_ref.dtype)
            elif first:
                acc_ref[rows, :] = term
            else:
                acc_ref[rows, :] += term

    pl.when(j == 0)(lambda: step(True, False))
    pl.when(jnp.logical_and(j > 0, j < n_branch - 1))(lambda: step(False, False))
    pl.when(j == n_branch - 1)(lambda: step(False, True))


def _merge(xb, att, w_gate, w_br, layer, tm, n_chunks):
    rows = xb.shape[0]
    n_branch = 3
    return pl.pallas_call(
        functools.partial(_merge_body, n_branch=n_branch, n_chunks=n_chunks),
        grid=(rows // tm, n_branch),
        in_specs=[
            pl.BlockSpec((tm, D_MODEL), lambda i, j: (i, 0)),
            pl.BlockSpec((tm, A_WIDTH), lambda i, j: (i, j)),
            pl.BlockSpec((None, D_MODEL, D_MODEL), lambda i, j: (layer, 0, j)),
            pl.BlockSpec((None, None, A_WIDTH, D_MODEL), lambda i, j: (layer, j, 0, 0)),
        ],
        out_specs=pl.BlockSpec((tm, D_MODEL), lambda i, j: (i, 0)),
        out_shape=jax.ShapeDtypeStruct((rows, D_MODEL), BF16),
        scratch_shapes=[pltpu.VMEM((tm, D_MODEL), F32)],
        compiler_params=_params(("parallel", "arbitrary"), 52),
        name="merge",
    )(xb, att, w_gate, w_br)


def _outproj_body(x_ref, h_ref, w_ref, g_ref, b_ref, y_ref, yb_ref, *, n_chunks):
    for rows in _row_chunks(x_ref.shape[0], n_chunks):
        mix = jnp.dot(h_ref[rows, :], w_ref[...], preferred_element_type=F32)
        z = DEEPNORM_ALPHA * x_ref[rows, :] + mix
        _store_norm(z, g_ref, b_ref, y_ref, yb_ref, rows)


def _outproj(x, h, w_out, ln_g, ln_b, layer, tm, n_chunks):
    rows = x.shape[0]
    ln_row = layer * 3 + 1
    row_spec = pl.BlockSpec((tm, D_MODEL), lambda i: (i, 0))
    return pl.pallas_call(
        functools.partial(_outproj_body, n_chunks=n_chunks),
        grid=(rows // tm,),
        in_specs=[
            row_spec, row_spec,
            pl.BlockSpec((None, D_MODEL, D_MODEL), lambda i: (layer, 0, 0)),
            pl.BlockSpec((None, 1, D_MODEL), lambda i: (ln_row, 0, 0)),
            pl.BlockSpec((None, 1, D_MODEL), lambda i: (ln_row, 0, 0)),
        ],
        out_specs=[row_spec, row_spec],
        out_shape=[jax.ShapeDtypeStruct((rows, D_MODEL), F32), jax.ShapeDtypeStruct((rows, D_MODEL), BF16)],
        compiler_params=_params(("parallel",), 52),
        name="outproj",
    )(x, h, w_out, ln_g, ln_b)


def _toeplitz(row_of_delta, n_q, n_k):
    period = n_q + n_k
    k = np.arange(period)
    w = row_of_delta(np.where(k < n_k, k, k - period))
    return jnp.tile(w, (1, n_q))[:, :n_q * (period - 1)].reshape(-1, n_q, period - 1)[:, :, :n_k]


def _rel_table_bias(table, offset, n_q, n_k):
    def row(delta):
        idx = np.clip(offset - delta, -(CHUNK - 1), A_REL_MAX) + (CHUNK - 1)
        return table.astype(F32)[:, idx]
    return _toeplitz(row, n_q, n_k)


def _alibi(offset, n_q, n_k):
    slopes = 2.0 ** (-8.0 * (jnp.arange(B_HEADS, dtype=F32) + 1.0) / B_HEADS)
    rel = offset + np.arange(n_q)[:, None] - np.arange(n_k)[None, :]
    return -slopes[:, None, None] * jnp.asarray(np.abs(rel), dtype=F32)


def _band(n_keys, n_prev):
    cq = np.arange(QB)[:, None] // CHUNK
    ck = np.arange(n_keys)[None, :] // CHUNK
    return (ck >= cq) & (ck <= cq + n_prev)


def _prompt_biases(table):
    inv = F32(HEAD_DIM ** 0.5)
    n_a, n_b = 3 * QB, QB // 2 + QB
    v = np.arange(N_BIAS_VARIANTS)[:, None, None, None]
    exists_a = np.arange(n_a)[None, None, None, :] >= (N_BIAS_VARIANTS - 1 - v) * QB
    exists_b = (np.arange(n_b)[None, None, None, :] >= QB // 2) | (v >= 1)
    bias_a = _rel_table_bias(table, A_PREV_CHUNKS * CHUNK, QB, n_a) * inv
    bias_b = _alibi(B_PREV_CHUNKS * CHUNK, QB, n_b) * inv
    bias_a = jnp.where(_band(n_a, A_PREV_CHUNKS)[None, None] & exists_a, bias_a[None], NEG_INF)
    bias_b = jnp.where(_band(n_b, B_PREV_CHUNKS)[None, None] & exists_b, bias_b[None], NEG_INF)
    return bias_a, bias_b


def _sample_biases(table, wa, wb, t):
    inv = F32(HEAD_DIM ** 0.5)
    ba = _rel_table_bias(table, wa, t, wa + t) * inv
    bb = _alibi(wb, t, wb + t) * inv
    return ba[:, :, :wa], ba[:, :, wa:], bb[:, :, :wb], bb[:, :, wb:]


def kernel(x_prompt, x_sample, cache_a_k, cache_a_v, cache_b_k, cache_b_v, cache_mem_k, cache_mem_v,
           mem_prompt, w_in, w_br_a, w_br_b, w_br_m, w_out, w_mem_kv, rel_bias_a, sink_b,
           ffn1_gu, ffn1_down, ffn2_gu, ffn2_down, ln_g, ln_b):
    batch, seq, _ = x_prompt.shape
    streams, t, _ = x_sample.shape
    a_keep = min(A_PREV_CHUNKS * CHUNK, seq)
    b_keep = min(B_PREV_CHUNKS * CHUNK, seq)
    wa, wb = cache_a_k.shape[2], cache_b_k.shape[2]

    w_qkv = w_in[:, :, :QKV_WIDTH].astype(BF16)
    w_gate = w_in[:, :, QKV_WIDTH:].astype(BF16)
    w_br = jnp.stack([w_br_a, w_br_b, w_br_m], axis=1).astype(BF16)
    w_out_b = w_out.astype(BF16)
    w_mem_b = w_mem_kv.astype(BF16)
    f1gu, f1d = ffn1_gu.astype(BF16), ffn1_down.astype(BF16)
    f2gu, f2d = ffn2_gu.astype(BF16), ffn2_down.astype(BF16)
    lng = ln_g.reshape(DEPTH * 3, 1, D_MODEL)
    lnb = ln_b.reshape(DEPTH * 3, 1, D_MODEL)

    cak, cav = cache_a_k, cache_a_v
    cbk = cache_b_k.reshape(DEPTH, streams, wb, B_KV_WIDTH)
    cbv = cache_b_v.reshape(DEPTH, streams, wb, B_KV_WIDTH)
    cmk = cache_mem_k.reshape(DEPTH, streams, N_MEM, M_WIDTH)
    cmv = cache_mem_v.reshape(DEPTH, streams, N_MEM, M_WIDTH)

    xp = x_prompt.reshape(batch * seq, D_MODEL)
    xs = x_sample.reshape(streams * t, D_MODEL)
    xpb, xsb = xp.astype(BF16), xs.astype(BF16)
    memb = mem_prompt.reshape(batch * N_MEM, D_MODEL).astype(BF16)

    tm_p, ch_p = 512, 2
    tm_s = streams * t
    ffn_tiles_p = (2048, 4, 256, 2)
    ffn_tiles_s = (tm_s, 1, tm_s, 1)
    n_small_qkv = QKV_WIDTH // TN_SMALL
    n_small_mem = 2 * M_WIDTH // TN_SMALL
    kv_first, kv_b = COL_KA // TN_SMALL, COL_KB // TN_SMALL
    n_kv_a = 2 * A_WIDTH // TN_SMALL

    akp, avp, bkp, bvp, mkp, mvp = [], [], [], [], [], []
    aks, avs, bks, bvs = [], [], [], []
    for l in range(DEPTH):
        bias_a, bias_b = _prompt_biases(rel_bias_a[l])
        s_biases = _sample_biases(rel_bias_a[l], wa, wb, t)
        sink = sink_b[l].astype(F32) * F32(HEAD_DIM ** 0.5)

        xp, xpb = _ffn(xp, xpb, f1gu, f1d, lng, lnb, l, 0, *ffn_tiles_p)
        p = _proj(xpb, w_qkv, l, TN_QKV, lambda j: j, QKV_WIDTH // TN_QKV, 1024, 4, BF16)
        tail = xpb.reshape(batch, seq, D_MODEL)[:, seq - a_keep:].reshape(batch * a_keep, D_MODEL)
        kv = _proj(tail, w_qkv, l, TN_SMALL, lambda j: jnp.where(j < n_kv_a, j + kv_first, kv_b),
                   n_kv_a + 1, 512, 1, F32).reshape(batch, a_keep, -1)
        mkv = _proj(memb, w_mem_b, l, TN_SMALL, lambda j: j, n_small_mem, batch * N_MEM, 1, F32)
        mk, mv = mkv[:, :M_WIDTH], mkv[:, M_WIDTH:]
        att = _attn_prompt(p, mk.astype(BF16), mv.astype(BF16), bias_a, bias_b, sink, batch, seq)
        h = _merge(xpb, att, w_gate, w_br, l, tm_p, ch_p)
        xp, xpb = _outproj(xp, h, w_out_b, lng, lnb, l, tm_p, 4)
        xp, xpb = _ffn(xp, xpb, f2gu, f2d, lng, lnb, l, 2, *ffn_tiles_p)

        akp.append(kv[:, :, :A_WIDTH].reshape(batch, a_keep, A_HEADS, HEAD_DIM))
        avp.append(kv[:, :, A_WIDTH:2 * A_WIDTH].reshape(batch, a_keep, A_HEADS, HEAD_DIM))
        kvb = kv[:, a_keep - b_keep:, 2 * A_WIDTH:]
        bkp.append(kvb[:, :, :B_KV_WIDTH].reshape(batch, b_keep, B_KV_HEADS, HEAD_DIM))
        bvp.append(kvb[:, :, B_KV_WIDTH:].reshape(batch, b_keep, B_KV_HEADS, HEAD_DIM))
        mkp.append(mk.reshape(batch, N_MEM, M_HEADS, M_HEAD_DIM))
        mvp.append(mv.reshape(batch, N_MEM, M_HEADS, M_HEAD_DIM))

        xs, xsb = _ffn(xs, xsb, f1gu, f1d, lng, lnb, l, 0, *ffn_tiles_s)
        ps = _proj(xsb, w_qkv, l, TN_SMALL, lambda j: j, n_small_qkv, tm_s, 1, F32)
        att_s = _attn_sample(ps, cak, cav, cbk, cbv, cmk, cmv, s_biases, sink, l, streams, t)
        hs = _merge(xsb, att_s, w_gate, w_br, l, tm_s, 1)
        xs, xsb = _outproj(xs, hs, w_out_b, lng, lnb, l, tm_s, 1)
        xs, xsb = _ffn(xs, xsb, f2gu, f2d, lng, lnb, l, 2, *ffn_tiles_s)

        ps3 = ps.reshape(streams, t, QKV_WIDTH)
        aks.append(ps3[:, :, COL_KA:COL_KA + A_WIDTH].reshape(streams, t, A_HEADS, HEAD_DIM))
        avs.append(ps3[:, :, COL_VA:COL_VA + A_WIDTH].reshape(streams, t, A_HEADS, HEAD_DIM))
        bks.append(ps3[:, :, COL_KB:COL_KB + B_KV_WIDTH].reshape(streams, t, B_KV_HEADS, HEAD_DIM))
        bvs.append(ps3[:, :, COL_VB:COL_VB + B_KV_WIDTH].reshape(streams, t, B_KV_HEADS, HEAD_DIM))

    return (xp.reshape(batch, seq, D_MODEL), xs.reshape(streams, t, D_MODEL),
            jnp.stack(akp), jnp.stack(avp), jnp.stack(bkp), jnp.stack(bvp), jnp.stack(mkp), jnp.stack(mvp),
            jnp.stack(aks), jnp.stack(avs), jnp.stack(bks), jnp.stack(bvs))
```

```python
import functools

import numpy as np
import jax
import jax.numpy as jnp
from jax import lax
from jax.experimental import pallas as pl
from jax.experimental.pallas import tpu as pltpu

F32 = jnp.float32
BF16 = jnp.bfloat16

D_MODEL = 2048
DEPTH = 4
CHUNK = 64
HEAD_DIM = 128
A_HEADS = 8
A_PREV_CHUNKS = 8
A_REL_MAX = 128
B_HEADS = 8
B_KV_HEADS = 2
B_PREV_CHUNKS = 2
M_HEADS = 4
M_HEAD_DIM = 256
N_MEM = 256
D_FF = 5632

A_WIDTH = A_HEADS * HEAD_DIM
B_Q_WIDTH = B_HEADS * HEAD_DIM
B_KV_WIDTH = B_KV_HEADS * HEAD_DIM
M_WIDTH = M_HEADS * M_HEAD_DIM
QKV_WIDTH = 3 * A_WIDTH + B_Q_WIDTH + 2 * B_KV_WIDTH + M_WIDTH
COL_QA, COL_KA, COL_VA = 0, A_WIDTH, 2 * A_WIDTH
COL_QB = 3 * A_WIDTH
COL_KB = COL_QB + B_Q_WIDTH
COL_VB = COL_KB + B_KV_WIDTH
COL_QM = COL_VB + B_KV_WIDTH
ATT_WIDTH = A_WIDTH + B_Q_WIDTH + M_WIDTH

DEEPNORM_ALPHA = (2 * DEPTH) ** 0.25
LN_EPS = 1e-5
NEG_INF = -1e30

MIB = 1024 * 1024
QB = 4 * CHUNK
N_BIAS_VARIANTS = A_PREV_CHUNKS * CHUNK // QB + 1
MXU_WIDTH = 256
HEAD_GROUP = 4
TF = 512
TN_QKV = QKV_WIDTH // 2
TN_SMALL = 512


def _params(sem, vmem_mib):
    return pltpu.CompilerParams(dimension_semantics=sem, vmem_limit_bytes=vmem_mib * MIB)


def _layernorm(z, g, b):
    mu = jnp.mean(z, axis=-1, keepdims=True)
    zc = z - mu
    var = jnp.mean(zc * zc, axis=-1, keepdims=True)
    return zc * lax.rsqrt(var + LN_EPS) * g + b


def _row_chunks(tm, n_chunks):
    rc = tm // n_chunks
    return [slice(r * rc, (r + 1) * rc) for r in range(n_chunks)]


def _store_norm(z, g_ref, b_ref, y_ref, yb_ref, rows):
    y = _layernorm(z, g_ref[...], b_ref[...])
    y_ref[rows, :] = y
    yb_ref[rows, :] = y.astype(BF16)


def _ffn_up_body(xb_ref, wg_ref, wu_ref, h_ref, *, n_chunks):
    for rows in _row_chunks(xb_ref.shape[0], n_chunks):
        xb = xb_ref[rows, :]
        gate = jnp.dot(xb, wg_ref[...], preferred_element_type=F32)
        up = jnp.dot(xb, wu_ref[...], preferred_element_type=F32)
        h_ref[rows, :] = (gate * jax.nn.sigmoid(gate) * up).astype(h_ref.dtype)


def _residual_norm_chunks(x_ref, lhs_ref, w_ref, g_ref, b_ref, y_ref, yb_ref, n_chunks, branch_weight):
    for rows in _row_chunks(x_ref.shape[0], n_chunks):
        d = jnp.dot(lhs_ref[rows, :], w_ref[...], preferred_element_type=F32)
        branch = d if branch_weight == 1.0 else branch_weight * d
        _store_norm(DEEPNORM_ALPHA * x_ref[rows, :] + branch, g_ref, b_ref, y_ref, yb_ref, rows)


def _ffn_down_body(x_ref, h_ref, wd_ref, g_ref, b_ref, y_ref, yb_ref, *, n_chunks):
    _residual_norm_chunks(x_ref, h_ref, wd_ref, g_ref, b_ref, y_ref, yb_ref, n_chunks, 0.5)


def _ffn(x, xb, w_gu, w_down, ln_g, ln_b, layer, ln_idx, tm_up, ch_up, tm_down, ch_down):
    rows = x.shape[0]
    nj = D_FF // TF
    ln_row = layer * 3 + ln_idx
    hidden = pl.pallas_call(
        functools.partial(_ffn_up_body, n_chunks=ch_up),
        grid=(rows // tm_up, nj),
        in_specs=[
            pl.BlockSpec((tm_up, D_MODEL), lambda i, j: (i, 0)),
            pl.BlockSpec((None, D_MODEL, TF), lambda i, j: (layer, 0, j)),
            pl.BlockSpec((None, D_MODEL, TF), lambda i, j: (layer, 0, j + nj)),
        ],
        out_specs=pl.BlockSpec((tm_up, TF), lambda i, j: (i, j)),
        out_shape=jax.ShapeDtypeStruct((rows, D_FF), BF16),
        compiler_params=_params(("parallel", "arbitrary"), 48),
        name="ffn_up",
    )(xb, w_gu, w_gu)
    row_spec = pl.BlockSpec((tm_down, D_MODEL), lambda i: (i, 0))
    ln_spec = pl.BlockSpec((None, 1, D_MODEL), lambda i: (ln_row, 0, 0))
    return pl.pallas_call(
        functools.partial(_ffn_down_body, n_chunks=ch_down),
        grid=(rows // tm_down,),
        in_specs=[
            row_spec,
            pl.BlockSpec((tm_down, D_FF), lambda i: (i, 0)),
            pl.BlockSpec((None, D_FF, D_MODEL), lambda i: (layer, 0, 0), pipeline_mode=pl.Buffered(1)),
            ln_spec, ln_spec,
        ],
        out_specs=[row_spec, row_spec],
        out_shape=[jax.ShapeDtypeStruct((rows, D_MODEL), F32), jax.ShapeDtypeStruct((rows, D_MODEL), BF16)],
        compiler_params=_params(("parallel",), 52),
        name="ffn_down",
    )(x, hidden, w_down, ln_g, ln_b)


def _proj_body(xb_ref, w_ref, o_ref, *, n_chunks):
    for rows in _row_chunks(xb_ref.shape[0], n_chunks):
        o_ref[rows, :] = jnp.dot(xb_ref[rows, :], w_ref[...], preferred_element_type=F32).astype(o_ref.dtype)


def _proj(xb, w, layer, tn, col_block, n_col_blocks, tm, n_chunks, out_dtype):
    rows = xb.shape[0]
    return pl.pallas_call(
        functools.partial(_proj_body, n_chunks=n_chunks),
        grid=(rows // tm, n_col_blocks),
        in_specs=[
            pl.BlockSpec((tm, D_MODEL), lambda i, j: (i, 0)),
            pl.BlockSpec((None, D_MODEL, tn), lambda i, j: (layer, 0, col_block(j))),
        ],
        out_specs=pl.BlockSpec((tm, tn), lambda i, j: (i, j)),
        out_shape=jax.ShapeDtypeStruct((rows, n_col_blocks * tn), out_dtype),
        compiler_params=_params(("parallel", "arbitrary"), 52),
        name="proj",
    )(xb, w)


def _qk(q, k):
    return lax.dot_general(q, k, (((1,), (1,)), ((), ())), preferred_element_type=F32)


def _softmax_pv(scores, values, scale, sink=None):
    return _softmax_pv_group([(scores, values, sink)], scale)[0]


def _softmax_pv_group(heads, scale):
    c = F32(scale * np.log2(np.e))
    ms = []
    for scores, _, sink in heads:
        m = scores[0].max(axis=-1, keepdims=True)
        for s in scores[1:]:
            m = jnp.maximum(m, s.max(axis=-1, keepdims=True))
        ms.append(m if sink is None else jnp.maximum(m, sink))
    es = [[jnp.exp2((s - m) * c) for s in scores] for (scores, _, _), m in zip(heads, ms)]
    outs = []
    for (_, values, sink), m, e_blocks in zip(heads, ms, es):
        d = values[0].shape[-1]
        ones_cols = d < MXU_WIDTH
        o = None
        for e, v in zip(e_blocks, values):
            rhs = jnp.concatenate([v, jnp.ones_like(v)], axis=1) if ones_cols else v
            part = jnp.dot(e.astype(BF16), rhs, preferred_element_type=F32)
            o = part if o is None else o + part
        if ones_cols:
            o, denom = o[:, :d], o[:, d:]
        else:
            denom = e_blocks[0].sum(axis=-1, keepdims=True)
            for e in e_blocks[1:]:
                denom = denom + e.sum(axis=-1, keepdims=True)
        if sink is not None:
            denom = denom + jnp.exp2((sink - m) * c)
        outs.append(o / denom)
    return outs


def _hs(h, width=HEAD_DIM):
    return slice(h * width, (h + 1) * width)


def _attn_prompt_body(sink_ref, cur_ref, ka2_ref, ka1_ref, va2_ref, va1_ref, kbp_ref, vbp_ref,
                      mk_ref, mv_ref, bias_a_ref, bias_b_ref, o_ref):
    scale = HEAD_DIM ** -0.5
    rep = B_HEADS // B_KV_HEADS
    half = QB // 2

    def cur(c0, h, width=HEAD_DIM):
        return cur_ref[:, c0 + h * width:c0 + (h + 1) * width]

    def head_a(h):
        q = cur(COL_QA, h)
        t2 = _qk(q, ka2_ref[:, _hs(h)]) + bias_a_ref[h, :, 0:QB]
        t1 = _qk(q, ka1_ref[:, _hs(h)]) + bias_a_ref[h, :, QB:2 * QB]
        t0 = _qk(q, cur(COL_KA, h)) + bias_a_ref[h, :, 2 * QB:3 * QB]
        return [t2, t1, t0], [va2_ref[:, _hs(h)], va1_ref[:, _hs(h)], cur(COL_VA, h)], None

    def head_b(h):
        g = h // rep
        q = cur(COL_QB, h)
        t1 = _qk(q, kbp_ref[:, _hs(g)]) + bias_b_ref[h, :, 0:half]
        t0 = _qk(q, cur(COL_KB, g)) + bias_b_ref[h, :, half:half + QB]
        return [t1, t0], [vbp_ref[:, _hs(g)], cur(COL_VB, g)], sink_ref[h]

    def head_m(h):
        t = _qk(cur(COL_QM, h, M_HEAD_DIM), mk_ref[:, _hs(h, M_HEAD_DIM)])
        return [t], [mv_ref[:, _hs(h, M_HEAD_DIM)]], None

    branches = ((head_a, A_HEADS, HEAD_DIM, scale, 0), (head_b, B_HEADS, HEAD_DIM, scale, A_WIDTH),
                (head_m, M_HEADS, M_HEAD_DIM, M_HEAD_DIM ** -0.5, A_WIDTH + B_Q_WIDTH))
    for head, n_heads, width, sc, base in branches:
        group = min(HEAD_GROUP, n_heads)
        for h0 in range(0, n_heads, group):
            outs = _softmax_pv_group([head(h) for h in range(h0, h0 + group)], sc)
            for h, o in zip(range(h0, h0 + group), outs):
                o_ref[:, base + h * width:base + (h + 1) * width] = o.astype(o_ref.dtype)


def _attn_prompt(p, mk, mv, bias_a, bias_b, sink, batch, seq):
    nb = seq // QB
    half = QB // 2
    ka_blk, va_blk = COL_KA // A_WIDTH, COL_VA // A_WIDTH
    kb_blk, vb_blk = COL_KB // B_KV_WIDTH, COL_VB // B_KV_WIDTH

    def prev_a(col, back):
        return pl.BlockSpec((QB, A_WIDTH), lambda b, i: (b * nb + jnp.maximum(i - back, 0), col))

    def prev_b(col):
        return pl.BlockSpec((half, B_KV_WIDTH),
                            lambda b, i: (b * 2 * nb + jnp.maximum(2 * i - 1, 0), col))

    return pl.pallas_call(
        _attn_prompt_body,
        grid=(batch, nb),
        in_specs=[
            pl.BlockSpec(memory_space=pltpu.SMEM),
            pl.BlockSpec((QB, QKV_WIDTH), lambda b, i: (b * nb + i, 0)),
            prev_a(ka_blk, 2), prev_a(ka_blk, 1), prev_a(va_blk, 2), prev_a(va_blk, 1),
            prev_b(kb_blk), prev_b(vb_blk),
            pl.BlockSpec((N_MEM, M_WIDTH), lambda b, i: (b, 0)),
            pl.BlockSpec((N_MEM, M_WIDTH), lambda b, i: (b, 0)),
            pl.BlockSpec((None, A_HEADS, QB, 3 * QB), lambda b, i: (jnp.minimum(i, N_BIAS_VARIANTS - 1), 0, 0, 0)),
            pl.BlockSpec((None, B_HEADS, QB, half + QB),
                         lambda b, i: (jnp.minimum(i, N_BIAS_VARIANTS - 1), 0, 0, 0)),
        ],
        out_specs=pl.BlockSpec((QB, ATT_WIDTH), lambda b, i: (b * nb + i, 0)),
        out_shape=jax.ShapeDtypeStruct((batch * seq, ATT_WIDTH), BF16),
        compiler_params=_params(("parallel", "arbitrary"), 48),
        name="attn_prompt",
    )(sink, p, p, p, p, p, p, p, mk, mv, bias_a, bias_b)


def _attn_sample_body(sink_ref, cur_ref, cak_ref, cav_ref, cbk_ref, cbv_ref, cmk_ref, cmv_ref,
                      bias_ac_ref, bias_an_ref, bias_bc_ref, bias_bn_ref, o_ref):
    scale = HEAD_DIM ** -0.5

    def col(c0, h, width=HEAD_DIM):
        return cur_ref[:, c0 + h * width:c0 + (h + 1) * width].astype(BF16)

    for h in range(A_HEADS):
        q = col(COL_QA, h)
        ks = [cak_ref[:, h, :].astype(BF16), col(COL_KA, h)]
        vs = [cav_ref[:, h, :].astype(BF16), col(COL_VA, h)]
        tc = _qk(q, ks[0]) + bias_ac_ref[h]
        tn = _qk(q, ks[1]) + bias_an_ref[h]
        o_ref[:, _hs(h)] = _softmax_pv([tc, tn], vs, scale).astype(o_ref.dtype)

    rep = B_HEADS // B_KV_HEADS
    for h in range(B_HEADS):
        g = h // rep
        q = col(COL_QB, h)
        ks = [cbk_ref[:, _hs(g)].astype(BF16), col(COL_KB, g)]
        vs = [cbv_ref[:, _hs(g)].astype(BF16), col(COL_VB, g)]
        tc = _qk(q, ks[0]) + bias_bc_ref[h]
        tn = _qk(q, ks[1]) + bias_bn_ref[h]
        o = _softmax_pv([tc, tn], vs, scale, sink=sink_ref[h])
        o_ref[:, A_WIDTH + h * HEAD_DIM:A_WIDTH + (h + 1) * HEAD_DIM] = o.astype(o_ref.dtype)

    for h in range(M_HEADS):
        q = col(COL_QM, h, M_HEAD_DIM)
        t = _qk(q, cmk_ref[:, _hs(h, M_HEAD_DIM)].astype(BF16))
        o = _softmax_pv([t], [cmv_ref[:, _hs(h, M_HEAD_DIM)].astype(BF16)], M_HEAD_DIM ** -0.5)
        base = A_WIDTH + B_Q_WIDTH
        o_ref[:, base + h * M_HEAD_DIM:base + (h + 1) * M_HEAD_DIM] = o.astype(o_ref.dtype)


def _attn_sample(p, cak, cav, cbk, cbv, cmk, cmv, biases, sink, layer, streams, t):
    bias_ac, bias_an, bias_bc, bias_bn = biases

    def cache_spec(c):
        return pl.BlockSpec((None, None) + c.shape[2:], lambda s: (layer, s) + (0,) * (c.ndim - 2))

    def full_spec(a):
        return pl.BlockSpec(a.shape, lambda s: (0, 0, 0))

    return pl.pallas_call(
        _attn_sample_body,
        grid=(streams,),
        in_specs=[
            pl.BlockSpec(memory_space=pltpu.SMEM),
            pl.BlockSpec((t, QKV_WIDTH), lambda s: (s, 0)),
            cache_spec(cak), cache_spec(cav), cache_spec(cbk), cache_spec(cbv),
            cache_spec(cmk), cache_spec(cmv),
            full_spec(bias_ac), full_spec(bias_an), full_spec(bias_bc), full_spec(bias_bn),
        ],
        out_specs=pl.BlockSpec((t, ATT_WIDTH), lambda s: (s, 0)),
        out_shape=jax.ShapeDtypeStruct((streams * t, ATT_WIDTH), BF16),
        compiler_params=_params(("arbitrary",), 32),
        name="attn_sample",
    )(sink, p, cak, cav, cbk, cbv, cmk, cmv, bias_ac, bias_an, bias_bc, bias_bn)


def _merge_body(xb_ref, o_ref, wg_ref, wbr_ref, h_ref, acc_ref, *, n_branch, n_chunks):
    j = pl.program_id(1)

    def step(first, last):
        for rows in _row_chunks(xb_ref.shape[0], n_chunks):
            gate = jax.nn.sigmoid(jnp.dot(xb_ref[rows, :], wg_ref[...], preferred_element_type=F32))
            term = gate * jnp.dot(o_ref[rows, :], wbr_ref[...], preferred_element_type=F32)
            if last:
                h_ref[rows, :] = (acc_ref[rows, :] + term).astype(h_ref.dtype)
            elif first:
                acc_ref[rows, :] = term
            else:
                acc_ref[rows, :] += term

    pl.when(j == 0)(lambda: step(True, False))
    pl.when(jnp.logical_and(j > 0, j < n_branch - 1))(lambda: step(False, False))
    pl.when(j == n_branch - 1)(lambda: step(False, True))


def _merge(xb, att, w_gate, w_br, layer, tm, n_chunks):
    rows = xb.shape[0]
    n_branch = 3
    return pl.pallas_call(
        functools.partial(_merge_body, n_branch=n_branch, n_chunks=n_chunks),
        grid=(rows // tm, n_branch),
        in_specs=[
            pl.BlockSpec((tm, D_MODEL), lambda i, j: (i, 0)),
            pl.BlockSpec((tm, A_WIDTH), lambda i, j: (i, j)),
            pl.BlockSpec((None, D_MODEL, D_MODEL), lambda i, j: (layer, 0, j)),
            pl.BlockSpec((None, None, A_WIDTH, D_MODEL), lambda i, j: (layer, j, 0, 0)),
        ],
        out_specs=pl.BlockSpec((tm, D_MODEL), lambda i, j: (i, 0)),
        out_shape=jax.ShapeDtypeStruct((rows, D_MODEL), BF16),
        scratch_shapes=[pltpu.VMEM((tm, D_MODEL), F32)],
        compiler_params=_params(("parallel", "arbitrary"), 52),
        name="merge",
    )(xb, att, w_gate, w_br)


def _outproj_body(x_ref, h_ref, w_ref, g_ref, b_ref, y_ref, yb_ref, *, n_chunks):
    _residual_norm_chunks(x_ref, h_ref, w_ref, g_ref, b_ref, y_ref, yb_ref, n_chunks, 1.0)


def _outproj(x, h, w_out, ln_g, ln_b, layer, tm, n_chunks):
    rows = x.shape[0]
    ln_row = layer * 3 + 1
    row_spec = pl.BlockSpec((tm, D_MODEL), lambda i: (i, 0))
    return pl.pallas_call(
        functools.partial(_outproj_body, n_chunks=n_chunks),
        grid=(rows // tm,),
        in_specs=[
            row_spec, row_spec,
            pl.BlockSpec((None, D_MODEL, D_MODEL), lambda i: (layer, 0, 0)),
            pl.BlockSpec((None, 1, D_MODEL), lambda i: (ln_row, 0, 0)),
            pl.BlockSpec((None, 1, D_MODEL), lambda i: (ln_row, 0, 0)),
        ],
        out_specs=[row_spec, row_spec],
        out_shape=[jax.ShapeDtypeStruct((rows, D_MODEL), F32), jax.ShapeDtypeStruct((rows, D_MODEL), BF16)],
        compiler_params=_params(("parallel",), 52),
        name="outproj",
    )(x, h, w_out, ln_g, ln_b)


def _toeplitz(row_of_delta, n_q, n_k):
    period = n_q + n_k
    k = np.arange(period)
    w = row_of_delta(np.where(k < n_k, k, k - period))
    return jnp.tile(w, (1, n_q))[:, :n_q * (period - 1)].reshape(-1, n_q, period - 1)[:, :, :n_k]


def _rel_table_bias(table, offset, n_q, n_k):
    def row(delta):
        idx = np.clip(offset - delta, -(CHUNK - 1), A_REL_MAX) + (CHUNK - 1)
        return table.astype(F32)[:, idx]
    return _toeplitz(row, n_q, n_k)


def _alibi(offset, n_q, n_k):
    slopes = 2.0 ** (-8.0 * (jnp.arange(B_HEADS, dtype=F32) + 1.0) / B_HEADS)
    rel = offset + np.arange(n_q)[:, None] - np.arange(n_k)[None, :]
    return -slopes[:, None, None] * jnp.asarray(np.abs(rel), dtype=F32)


def _band(n_keys, n_prev):
    cq = np.arange(QB)[:, None] // CHUNK
    ck = np.arange(n_keys)[None, :] // CHUNK
    return (ck >= cq) & (ck <= cq + n_prev)


def _prompt_biases(table):
    inv = F32(HEAD_DIM ** 0.5)
    n_a, n_b = 3 * QB, QB // 2 + QB
    v = np.arange(N_BIAS_VARIANTS)[:, None, None, None]
    exists_a = np.arange(n_a)[None, None, None, :] >= (N_BIAS_VARIANTS - 1 - v) * QB
    exists_b = (np.arange(n_b)[None, None, None, :] >= QB // 2) | (v >= 1)
    bias_a = _rel_table_bias(table, A_PREV_CHUNKS * CHUNK, QB, n_a) * inv
    bias_b = _alibi(B_PREV_CHUNKS * CHUNK, QB, n_b) * inv
    bias_a = jnp.where(_band(n_a, A_PREV_CHUNKS)[None, None] & exists_a, bias_a[None], NEG_INF)
    bias_b = jnp.where(_band(n_b, B_PREV_CHUNKS)[None, None] & exists_b, bias_b[None], NEG_INF)
    return bias_a, bias_b


def _sample_biases(table, wa, wb, t):
    inv = F32(HEAD_DIM ** 0.5)
    ba = _rel_table_bias(table, wa, t, wa + t) * inv
    bb = _alibi(wb, t, wb + t) * inv
    return ba[:, :, :wa], ba[:, :, wa:], bb[:, :, :wb], bb[:, :, wb:]


def kernel(x_prompt, x_sample, cache_a_k, cache_a_v, cache_b_k, cache_b_v, cache_mem_k, cache_mem_v,
           mem_prompt, w_in, w_br_a, w_br_b, w_br_m, w_out, w_mem_kv, rel_bias_a, sink_b,
           ffn1_gu, ffn1_down, ffn2_gu, ffn2_down, ln_g, ln_b):
    batch, seq, _ = x_prompt.shape
    streams, t, _ = x_sample.shape
    a_keep = min(A_PREV_CHUNKS * CHUNK, seq)
    b_keep = min(B_PREV_CHUNKS * CHUNK, seq)
    wa, wb = cache_a_k.shape[2], cache_b_k.shape[2]

    w_qkv = w_in[:, :, :QKV_WIDTH].astype(BF16)
    w_gate = w_in[:, :, QKV_WIDTH:].astype(BF16)
    w_br = jnp.stack([w_br_a, w_br_b, w_br_m], axis=1).astype(BF16)
    w_out_b = w_out.astype(BF16)
    w_mem_b = w_mem_kv.astype(BF16)
    f1gu, f1d = ffn1_gu.astype(BF16), ffn1_down.astype(BF16)
    f2gu, f2d = ffn2_gu.astype(BF16), ffn2_down.astype(BF16)
    lng = ln_g.reshape(DEPTH * 3, 1, D_MODEL)
    lnb = ln_b.reshape(DEPTH * 3, 1, D_MODEL)

    cak, cav = cache_a_k, cache_a_v
    cbk = cache_b_k.reshape(DEPTH, streams, wb, B_KV_WIDTH)
    cbv = cache_b_v.reshape(DEPTH, streams, wb, B_KV_WIDTH)
    cmk = cache_mem_k.reshape(DEPTH, streams, N_MEM, M_WIDTH)
    cmv = cache_mem_v.reshape(DEPTH, streams, N_MEM, M_WIDTH)

    xp = x_prompt.reshape(batch * seq, D_MODEL)
    xs = x_sample.reshape(streams * t, D_MODEL)
    xpb, xsb = xp.astype(BF16), xs.astype(BF16)
    memb = mem_prompt.reshape(batch * N_MEM, D_MODEL).astype(BF16)

    tm_p, ch_p = 512, 2
    tm_s = streams * t
    ffn_tiles_p = (2048, 4, 256, 2)
    ffn_tiles_s = (tm_s, 1, tm_s, 1)
    n_small_qkv = QKV_WIDTH // TN_SMALL
    n_small_mem = 2 * M_WIDTH // TN_SMALL
    kv_first, kv_b = COL_KA // TN_SMALL, COL_KB // TN_SMALL
    n_kv_a = 2 * A_WIDTH // TN_SMALL

    akp, avp, bkp, bvp, mkp, mvp = [], [], [], [], [], []
    aks, avs, bks, bvs = [], [], [], []
    for l in range(DEPTH):
        bias_a, bias_b = _prompt_biases(rel_bias_a[l])
        s_biases = _sample_biases(rel_bias_a[l], wa, wb, t)
        sink = sink_b[l].astype(F32) * F32(HEAD_DIM ** 0.5)

        xp, xpb = _ffn(xp, xpb, f1gu, f1d, lng, lnb, l, 0, *ffn_tiles_p)
        p = _proj(xpb, w_qkv, l, TN_QKV, lambda j: j, QKV_WIDTH // TN_QKV, 1024, 4, BF16)
        tail = xpb.reshape(batch, seq, D_MODEL)[:, seq - a_keep:].reshape(batch * a_keep, D_MODEL)
        kv = _proj(tail, w_qkv, l, TN_SMALL, lambda j: jnp.where(j < n_kv_a, j + kv_first, kv_b),
                   n_kv_a + 1, 512, 1, F32).reshape(batch, a_keep, -1)
        mkv = _proj(memb, w_mem_b, l, TN_SMALL, lambda j: j, n_small_mem, batch * N_MEM, 1, F32)
        mk, mv = mkv[:, :M_WIDTH], mkv[:, M_WIDTH:]
        att = _attn_prompt(p, mk.astype(BF16), mv.astype(BF16), bias_a, bias_b, sink, batch, seq)
        h = _merge(xpb, att, w_gate, w_br, l, tm_p, ch_p)
        xp, xpb = _outproj(xp, h, w_out_b, lng, lnb, l, tm_p, 4)
        xp, xpb = _ffn(xp, xpb, f2gu, f2d, lng, lnb, l, 2, *ffn_tiles_p)

        akp.append(kv[:, :, :A_WIDTH].reshape(batch, a_keep, A_HEADS, HEAD_DIM))
        avp.append(kv[:, :, A_WIDTH:2 * A_WIDTH].reshape(batch, a_keep, A_HEADS, HEAD_DIM))
        kvb = kv[:, a_keep - b_keep:, 2 * A_WIDTH:]
        bkp.append(kvb[:, :, :B_KV_WIDTH].reshape(batch, b_keep, B_KV_HEADS, HEAD_DIM))
        bvp.append(kvb[:, :, B_KV_WIDTH:].reshape(batch, b_keep, B_KV_HEADS, HEAD_DIM))
        mkp.append(mk.reshape(batch, N_MEM, M_HEADS, M_HEAD_DIM))
        mvp.append(mv.reshape(batch, N_MEM, M_HEADS, M_HEAD_DIM))

        xs, xsb = _ffn(xs, xsb, f1gu, f1d, lng, lnb, l, 0, *ffn_tiles_s)
        ps = _proj(xsb, w_qkv, l, TN_SMALL, lambda j: j, n_small_qkv, tm_s, 1, F32)
        att_s = _attn_sample(ps, cak, cav, cbk, cbv, cmk, cmv, s_biases, sink, l, streams, t)
        hs = _merge(xsb, att_s, w_gate, w_br, l, tm_s, 1)
        xs, xsb = _outproj(xs, hs, w_out_b, lng, lnb, l, tm_s, 1)
        xs, xsb = _ffn(xs, xsb, f2gu, f2d, lng, lnb, l, 2, *ffn_tiles_s)

        ps3 = ps.reshape(streams, t, QKV_WIDTH)
        aks.append(ps3[:, :, COL_KA:COL_KA + A_WIDTH].reshape(streams, t, A_HEADS, HEAD_DIM))
        avs.append(ps3[:, :, COL_VA:COL_VA + A_WIDTH].reshape(streams, t, A_HEADS, HEAD_DIM))
        bks.append(ps3[:, :, COL_KB:COL_KB + B_KV_WIDTH].reshape(streams, t, B_KV_HEADS, HEAD_DIM))
        bvs.append(ps3[:, :, COL_VB:COL_VB + B_KV_WIDTH].reshape(streams, t, B_KV_HEADS, HEAD_DIM))

    return (xp.reshape(batch, seq, D_MODEL), xs.reshape(streams, t, D_MODEL),
            jnp.stack(akp), jnp.stack(avp), jnp.stack(bkp), jnp.stack(bvp), jnp.stack(mkp), jnp.stack(mvp),
            jnp.stack(aks), jnp.stack(avs), jnp.stack(bks), jnp.stack(bvs))
```

```python
import functools

import numpy as np
import jax
import jax.numpy as jnp
from jax import lax
from jax.experimental import pallas as pl
from jax.experimental.pallas import tpu as pltpu

F32 = jnp.float32
BF16 = jnp.bfloat16

D_MODEL = 2048
DEPTH = 4
CHUNK = 64
HEAD_DIM = 128
A_HEADS = 8
A_PREV_CHUNKS = 8
A_REL_MAX = 128
B_HEADS = 8
B_KV_HEADS = 2
B_PREV_CHUNKS = 2
M_HEADS = 4
M_HEAD_DIM = 256
N_MEM = 256
D_FF = 5632

A_WIDTH = A_HEADS * HEAD_DIM
B_Q_WIDTH = B_HEADS * HEAD_DIM
B_KV_WIDTH = B_KV_HEADS * HEAD_DIM
M_WIDTH = M_HEADS * M_HEAD_DIM
QKV_WIDTH = 3 * A_WIDTH + B_Q_WIDTH + 2 * B_KV_WIDTH + M_WIDTH
COL_QA, COL_KA, COL_VA = 0, A_WIDTH, 2 * A_WIDTH
COL_QB = 3 * A_WIDTH
COL_KB = COL_QB + B_Q_WIDTH
COL_VB = COL_KB + B_KV_WIDTH
COL_QM = COL_VB + B_KV_WIDTH
ATT_WIDTH = A_WIDTH + B_Q_WIDTH + M_WIDTH

DEEPNORM_ALPHA = (2 * DEPTH) ** 0.25
LN_EPS = 1e-5
NEG_INF = -1e30

MIB = 1024 * 1024
QB = 4 * CHUNK
N_BIAS_VARIANTS = A_PREV_CHUNKS * CHUNK // QB + 1
MXU_WIDTH = 256
HEAD_GROUP = 4
TF = 512
TN_QKV = QKV_WIDTH // 2
TN_SMALL = 512


def _params(sem, vmem_mib):
    return pltpu.CompilerParams(dimension_semantics=sem, vmem_limit_bytes=vmem_mib * MIB)


def _layernorm(z, g, b):
    mu = jnp.mean(z, axis=-1, keepdims=True)
    zc = z - mu
    var = jnp.mean(zc * zc, axis=-1, keepdims=True)
    return zc * lax.rsqrt(var + LN_EPS) * g + b


def _row_chunks(tm, n_chunks):
    rc = tm // n_chunks
    return [slice(r * rc, (r + 1) * rc) for r in range(n_chunks)]


def _store_norm(z, g_ref, b_ref, y_ref, yb_ref, rows):
    y = _layernorm(z, g_ref[...], b_ref[...])
    y_ref[rows, :] = y
    yb_ref[rows, :] = y.astype(BF16)


def _ffn_up_body(xb_ref, wg_ref, wu_ref, h_ref, *, n_chunks):
    for rows in _row_chunks(xb_ref.shape[0], n_chunks):
        xb = xb_ref[rows, :]
        gate = jnp.dot(xb, wg_ref[...], preferred_element_type=F32)
        up = jnp.dot(xb, wu_ref[...], preferred_element_type=F32)
        h_ref[rows, :] = (gate * jax.nn.sigmoid(gate) * up).astype(h_ref.dtype)


def _residual_norm_chunks(x_ref, lhs_ref, w_ref, g_ref, b_ref, y_ref, yb_ref, n_chunks, branch_weight):
    for rows in _row_chunks(x_ref.shape[0], n_chunks):
        d = jnp.dot(lhs_ref[rows, :], w_ref[...], preferred_element_type=F32)
        branch = d if branch_weight == 1.0 else branch_weight * d
        _store_norm(DEEPNORM_ALPHA * x_ref[rows, :] + branch, g_ref, b_ref, y_ref, yb_ref, rows)


def _ffn_down_body(x_ref, h_ref, wd_ref, g_ref, b_ref, y_ref, yb_ref, *, n_chunks):
    _residual_norm_chunks(x_ref, h_ref, wd_ref, g_ref, b_ref, y_ref, yb_ref, n_chunks, 0.5)


def _ffn(x, xb, w_gu, w_down, ln_g, ln_b, layer, ln_idx, tm_up, ch_up, tm_down, ch_down):
    rows = x.shape[0]
    nj = D_FF // TF
    ln_row = layer * 3 + ln_idx
    hidden = pl.pallas_call(
        functools.partial(_ffn_up_body, n_chunks=ch_up),
        grid=(rows // tm_up, nj),
        in_specs=[
            pl.BlockSpec((tm_up, D_MODEL), lambda i, j: (i, 0)),
            pl.BlockSpec((None, D_MODEL, TF), lambda i, j: (layer, 0, j)),
            pl.BlockSpec((None, D_MODEL, TF), lambda i, j: (layer, 0, j + nj)),
        ],
        out_specs=pl.BlockSpec((tm_up, TF), lambda i, j: (i, j)),
        out_shape=jax.ShapeDtypeStruct((rows, D_FF), BF16),
        compiler_params=_params(("parallel", "arbitrary"), 48),
        name="ffn_up",
    )(xb, w_gu, w_gu)
    row_spec = pl.BlockSpec((tm_down, D_MODEL), lambda i: (i, 0))
    ln_spec = pl.BlockSpec((None, 1, D_MODEL), lambda i: (ln_row, 0, 0))
    return pl.pallas_call(
        functools.partial(_ffn_down_body, n_chunks=ch_down),
        grid=(rows // tm_down,),
        in_specs=[
            row_spec,
            pl.BlockSpec((tm_down, D_FF), lambda i: (i, 0)),
            pl.BlockSpec((None, D_FF, D_MODEL), lambda i: (layer, 0, 0), pipeline_mode=pl.Buffered(1)),
            ln_spec, ln_spec,
        ],
        out_specs=[row_spec, row_spec],
        out_shape=[jax.ShapeDtypeStruct((rows, D_MODEL), F32), jax.ShapeDtypeStruct((rows, D_MODEL), BF16)],
        compiler_params=_params(("parallel",), 52),
        name="ffn_down",
    )(x, hidden, w_down, ln_g, ln_b)


def _proj_body(xb_ref, w_ref, o_ref, *, n_chunks):
    for rows in _row_chunks(xb_ref.shape[0], n_chunks):
        o_ref[rows, :] = jnp.dot(xb_ref[rows, :], w_ref[...], preferred_element_type=F32).astype(o_ref.dtype)


def _proj(xb, w, layer, tn, col_block, n_col_blocks, tm, n_chunks, out_dtype):
    rows = xb.shape[0]
    return pl.pallas_call(
        functools.partial(_proj_body, n_chunks=n_chunks),
        grid=(rows // tm, n_col_blocks),
        in_specs=[
            pl.BlockSpec((tm, D_MODEL), lambda i, j: (i, 0)),
            pl.BlockSpec((None, D_MODEL, tn), lambda i, j: (layer, 0, col_block(j))),
        ],
        out_specs=pl.BlockSpec((tm, tn), lambda i, j: (i, j)),
        out_shape=jax.ShapeDtypeStruct((rows, n_col_blocks * tn), out_dtype),
        compiler_params=_params(("parallel", "arbitrary"), 52),
        name="proj",
    )(xb, w)


def _qk(q, k):
    return lax.dot_general(q, k, (((1,), (1,)), ((), ())), preferred_element_type=F32)


def _softmax_pv(scores, values, scale, sink=None):
    return _softmax_pv_group([(scores, values, sink)], scale)[0]


def _softmax_pv_group(heads, scale):
    c = F32(scale * np.log2(np.e))
    ms = []
    for scores, _, sink in heads:
        m = scores[0].max(axis=-1, keepdims=True)
        for s in scores[1:]:
            m = jnp.maximum(m, s.max(axis=-1, keepdims=True))
        ms.append(m if sink is None else jnp.maximum(m, sink))
    es = [[jnp.exp2((s - m) * c) for s in scores] for (scores, _, _), m in zip(heads, ms)]
    outs = []
    for (_, values, sink), m, e_blocks in zip(heads, ms, es):
        d = values[0].shape[-1]
        ones_cols = d < MXU_WIDTH
        o = None
        for e, v in zip(e_blocks, values):
            rhs = jnp.concatenate([v, jnp.ones_like(v)], axis=1) if ones_cols else v
            part = jnp.dot(e.astype(BF16), rhs, preferred_element_type=F32)
            o = part if o is None else o + part
        if ones_cols:
            o, denom = o[:, :d], o[:, d:]
        else:
            denom = e_blocks[0].sum(axis=-1, keepdims=True)
            for e in e_blocks[1:]:
                denom = denom + e.sum(axis=-1, keepdims=True)
        if sink is not None:
            denom = denom + jnp.exp2((sink - m) * c)
        outs.append(o / denom)
    return outs


def _hs(h, width=HEAD_DIM):
    return slice(h * width, (h + 1) * width)


def _attn_prompt_body(sink_ref, cur_ref, ka2_ref, ka1_ref, va2_ref, va1_ref, kbp_ref, vbp_ref,
                      mk_ref, mv_ref, bias_a_ref, bias_b_ref, o_ref):
    scale = HEAD_DIM ** -0.5
    rep = B_HEADS // B_KV_HEADS
    half = QB // 2

    def cur(c0, h, width=HEAD_DIM):
        return cur_ref[:, c0 + h * width:c0 + (h + 1) * width]

    def head_a(h):
        q = cur(COL_QA, h)
        t2 = _qk(q, ka2_ref[:, _hs(h)]) + bias_a_ref[h, :, 0:QB]
        t1 = _qk(q, ka1_ref[:, _hs(h)]) + bias_a_ref[h, :, QB:2 * QB]
        t0 = _qk(q, cur(COL_KA, h)) + bias_a_ref[h, :, 2 * QB:3 * QB]
        return [t2, t1, t0], [va2_ref[:, _hs(h)], va1_ref[:, _hs(h)], cur(COL_VA, h)], None

    def head_b(h):
        g = h // rep
        q = cur(COL_QB, h)
        t1 = _qk(q, kbp_ref[:, _hs(g)]) + bias_b_ref[h, :, 0:half]
        t0 = _qk(q, cur(COL_KB, g)) + bias_b_ref[h, :, half:half + QB]
        return [t1, t0], [vbp_ref[:, _hs(g)], cur(COL_VB, g)], sink_ref[h]

    def head_m(h):
        t = _qk(cur(COL_QM, h, M_HEAD_DIM), mk_ref[:, _hs(h, M_HEAD_DIM)])
        return [t], [mv_ref[:, _hs(h, M_HEAD_DIM)]], None

    branches = ((head_a, A_HEADS, HEAD_DIM, scale, 0), (head_b, B_HEADS, HEAD_DIM, scale, A_WIDTH),
                (head_m, M_HEADS, M_HEAD_DIM, M_HEAD_DIM ** -0.5, A_WIDTH + B_Q_WIDTH))
    for head, n_heads, width, sc, base in branches:
        group = min(HEAD_GROUP, n_heads)
        for h0 in range(0, n_heads, group):
            outs = _softmax_pv_group([head(h) for h in range(h0, h0 + group)], sc)
            for h, o in zip(range(h0, h0 + group), outs):
                o_ref[:, base + h * width:base + (h + 1) * width] = o.astype(o_ref.dtype)


def _attn_prompt(p, mk, mv, bias_a, bias_b, sink, batch, seq):
    nb = seq // QB
    half = QB // 2
    ka_blk, va_blk = COL_KA // A_WIDTH, COL_VA // A_WIDTH
    kb_blk, vb_blk = COL_KB // B_KV_WIDTH, COL_VB // B_KV_WIDTH

    def prev_a(col, back):
        return pl.BlockSpec((QB, A_WIDTH), lambda b, i: (b * nb + jnp.maximum(i - back, 0), col))

    def prev_b(col):
        return pl.BlockSpec((half, B_KV_WIDTH),
                            lambda b, i: (b * 2 * nb + jnp.maximum(2 * i - 1, 0), col))

    return pl.pallas_call(
        _attn_prompt_body,
        grid=(batch, nb),
        in_specs=[
            pl.BlockSpec(memory_space=pltpu.SMEM),
            pl.BlockSpec((QB, QKV_WIDTH), lambda b, i: (b * nb + i, 0)),
            prev_a(ka_blk, 2), prev_a(ka_blk, 1), prev_a(va_blk, 2), prev_a(va_blk, 1),
            prev_b(kb_blk), prev_b(vb_blk),
            pl.BlockSpec((N_MEM, M_WIDTH), lambda b, i: (b, 0)),
            pl.BlockSpec((N_MEM, M_WIDTH), lambda b, i: (b, 0)),
            pl.BlockSpec((None, A_HEADS, QB, 3 * QB), lambda b, i: (jnp.minimum(i, N_BIAS_VARIANTS - 1), 0, 0, 0)),
            pl.BlockSpec((None, B_HEADS, QB, half + QB),
                         lambda b, i: (jnp.minimum(i, N_BIAS_VARIANTS - 1), 0, 0, 0)),
        ],
        out_specs=pl.BlockSpec((QB, ATT_WIDTH), lambda b, i: (b * nb + i, 0)),
        out_shape=jax.ShapeDtypeStruct((batch * seq, ATT_WIDTH), BF16),
        compiler_params=_params(("parallel", "arbitrary"), 48),
        name="attn_prompt",
    )(sink, p, p, p, p, p, p, p, mk, mv, bias_a, bias_b)


def _attn_sample_body(sink_ref, cur_ref, cak_ref, cav_ref, cbk_ref, cbv_ref, cmk_ref, cmv_ref,
                      bias_ac_ref, bias_an_ref, bias_bc_ref, bias_bn_ref, o_ref):
    scale = HEAD_DIM ** -0.5

    def col(c0, h, width=HEAD_DIM):
        return cur_ref[:, c0 + h * width:c0 + (h + 1) * width].astype(BF16)

    for h in range(A_HEADS):
        q = col(COL_QA, h)
        ks = [cak_ref[:, h, :].astype(BF16), col(COL_KA, h)]
        vs = [cav_ref[:, h, :].astype(BF16), col(COL_VA, h)]
        tc = _qk(q, ks[0]) + bias_ac_ref[h]
        tn = _qk(q, ks[1]) + bias_an_ref[h]
        o_ref[:, _hs(h)] = _softmax_pv([tc, tn], vs, scale).astype(o_ref.dtype)

    rep = B_HEADS // B_KV_HEADS
    for h in range(B_HEADS):
        g = h // rep
        q = col(COL_QB, h)
        ks = [cbk_ref[:, _hs(g)].astype(BF16), col(COL_KB, g)]
        vs = [cbv_ref[:, _hs(g)].astype(BF16), col(COL_VB, g)]
        tc = _qk(q, ks[0]) + bias_bc_ref[h]
        tn = _qk(q, ks[1]) + bias_bn_ref[h]
        o = _softmax_pv([tc, tn], vs, scale, sink=sink_ref[h])
        o_ref[:, A_WIDTH + h * HEAD_DIM:A_WIDTH + (h + 1) * HEAD_DIM] = o.astype(o_ref.dtype)

    for h in range(M_HEADS):
        q = col(COL_QM, h, M_HEAD_DIM)
        t = _qk(q, cmk_ref[:, _hs(h, M_HEAD_DIM)].astype(BF16))
        o = _softmax_pv([t], [cmv_ref[:, _hs(h, M_HEAD_DIM)].astype(BF16)], M_HEAD_DIM ** -0.5)
        base = A_WIDTH + B_Q_WIDTH
        o_ref[:, base + h * M_HEAD_DIM:base + (h + 1) * M_HEAD_DIM] = o.astype(o_ref.dtype)


def _attn_sample(p, cak, cav, cbk, cbv, cmk, cmv, biases, sink, layer, streams, t):
    bias_ac, bias_an, bias_bc, bias_bn = biases

    def cache_spec(c):
        return pl.BlockSpec((None, None) + c.shape[2:], lambda s: (layer, s) + (0,) * (c.ndim - 2))

    def full_spec(a):
        return pl.BlockSpec(a.shape, lambda s: (0, 0, 0))

    return pl.pallas_call(
        _attn_sample_body,
        grid=(streams,),
        in_specs=[
            pl.BlockSpec(memory_space=pltpu.SMEM),
            pl.BlockSpec((t, QKV_WIDTH), lambda s: (s, 0)),
            cache_spec(cak), cache_spec(cav), cache_spec(cbk), cache_spec(cbv),
            cache_spec(cmk), cache_spec(cmv),
            full_spec(bias_ac), full_spec(bias_an), full_spec(bias_bc), full_spec(bias_bn),
        ],
        out_specs=pl.BlockSpec((t, ATT_WIDTH), lambda s: (s, 0)),
        out_shape=jax.ShapeDtypeStruct((streams * t, ATT_WIDTH), BF16),
        compiler_params=_params(("arbitrary",), 32),
        name="attn_sample",
    )(sink, p, cak, cav, cbk, cbv, cmk, cmv, bias_ac, bias_an, bias_bc, bias_bn)


def _merge_body(xb_ref, o_ref, wg_ref, wbr_ref, h_ref, acc_ref, *, n_branch, n_chunks):
    j = pl.program_id(1)

    def step(first, last):
        for rows in _row_chunks(xb_ref.shape[0], n_chunks):
            gate = jax.nn.sigmoid(jnp.dot(xb_ref[rows, :], wg_ref[...], preferred_element_type=F32))
            term = gate * jnp.dot(o_ref[rows, :], wbr_ref[...], preferred_element_type=F32)
            if last:
                h_ref[rows, :] = (acc_ref[rows, :] + term).astype(h_ref.dtype)
            elif first:
                acc_ref[rows, :] = term
            else:
                acc_ref[rows, :] += term

    pl.when(j == 0)(lambda: step(True, False))
    pl.when(jnp.logical_and(j > 0, j < n_branch - 1))(lambda: step(False, False))
    pl.when(j == n_branch - 1)(lambda: step(False, True))


def _merge(xb, att, w_gate, w_br, layer, tm, n_chunks):
    rows = xb.shape[0]
    n_branch = 3
    return pl.pallas_call(
        functools.partial(_merge_body, n_branch=n_branch, n_chunks=n_chunks),
        grid=(rows // tm, n_branch),
        in_specs=[
            pl.BlockSpec((tm, D_MODEL), lambda i, j: (i, 0)),
            pl.BlockSpec((tm, A_WIDTH), lambda i, j: (i, j)),
            pl.BlockSpec((None, D_MODEL, D_MODEL), lambda i, j: (layer, 0, j)),
            pl.BlockSpec((None, None, A_WIDTH, D_MODEL), lambda i, j: (layer, j, 0, 0)),
        ],
        out_specs=pl.BlockSpec((tm, D_MODEL), lambda i, j: (i, 0)),
        out_shape=jax.ShapeDtypeStruct((rows, D_MODEL), BF16),
        scratch_shapes=[pltpu.VMEM((tm, D_MODEL), F32)],
        compiler_params=_params(("parallel", "arbitrary"), 52),
        name="merge",
    )(xb, att, w_gate, w_br)


def _outproj_body(x_ref, h_ref, w_ref, g_ref, b_ref, y_ref, yb_ref, *, n_chunks):
    _residual_norm_chunks(x_ref, h_ref, w_ref, g_ref, b_ref, y_ref, yb_ref, n_chunks, 1.0)


def _outproj(x, h, w_out, ln_g, ln_b, layer, tm, n_chunks):
    rows = x.shape[0]
    ln_row = layer * 3 + 1
    row_spec = pl.BlockSpec((tm, D_MODEL), lambda i: (i, 0))
    return pl.pallas_call(
        functools.partial(_outproj_body, n_chunks=n_chunks),
        grid=(rows // tm,),
        in_specs=[
            row_spec, row_spec,
            pl.BlockSpec((None, D_MODEL, D_MODEL), lambda i: (layer, 0, 0)),
            pl.BlockSpec((None, 1, D_MODEL), lambda i: (ln_row, 0, 0)),
            pl.BlockSpec((None, 1, D_MODEL), lambda i: (ln_row, 0, 0)),
        ],
        out_specs=[row_spec, row_spec],
        out_shape=[jax.ShapeDtypeStruct((rows, D_MODEL), F32), jax.ShapeDtypeStruct((rows, D_MODEL), BF16)],
        compiler_params=_params(("parallel",), 52),
        name="outproj",
    )(x, h, w_out, ln_g, ln_b)


def _toeplitz(row_of_delta, n_q, n_k):
    period = n_q + n_k
    k = np.arange(period)
    w = row_of_delta(np.where(k < n_k, k, k - period))
    return jnp.tile(w, (1, n_q))[:, :n_q * (period - 1)].reshape(-1, n_q, period - 1)[:, :, :n_k]


def _rel_table_bias(table, offset, n_q, n_k):
    def row(delta):
        idx = np.clip(offset - delta, -(CHUNK - 1), A_REL_MAX) + (CHUNK - 1)
        return table.astype(F32)[:, idx]
    return _toeplitz(row, n_q, n_k)


def _alibi(offset, n_q, n_k):
    slopes = 2.0 ** (-8.0 * (jnp.arange(B_HEADS, dtype=F32) + 1.0) / B_HEADS)
    rel = offset + np.arange(n_q)[:, None] - np.arange(n_k)[None, :]
    return -slopes[:, None, None] * jnp.asarray(np.abs(rel), dtype=F32)


def _band(n_keys, n_prev):
    cq = np.arange(QB)[:, None] // CHUNK
    ck = np.arange(n_keys)[None, :] // CHUNK
    return (ck >= cq) & (ck <= cq + n_prev)


def _prompt_biases(table):
    inv = F32(HEAD_DIM ** 0.5)
    n_a, n_b = 3 * QB, QB // 2 + QB
    v = np.arange(N_BIAS_VARIANTS)[:, None, None, None]
    exists_a = np.arange(n_a)[None, None, None, :] >= (N_BIAS_VARIANTS - 1 - v) * QB
    exists_b = (np.arange(n_b)[None, None, None, :] >= QB // 2) | (v >= 1)
    bias_a = _rel_table_bias(table, A_PREV_CHUNKS * CHUNK, QB, n_a) * inv
    bias_b = _alibi(B_PREV_CHUNKS * CHUNK, QB, n_b) * inv
    bias_a = jnp.where(_band(n_a, A_PREV_CHUNKS)[None, None] & exists_a, bias_a[None], NEG_INF)
    bias_b = jnp.where(_band(n_b, B_PREV_CHUNKS)[None, None] & exists_b, bias_b[None], NEG_INF)
    return bias_a, bias_b


def _sample_biases(table, wa, wb, t):
    inv = F32(HEAD_DIM ** 0.5)
    ba = _rel_table_bias(table, wa, t, wa + t) * inv
    bb = _alibi(wb, t, wb + t) * inv
    return ba[:, :, :wa], ba[:, :, wa:], bb[:, :, :wb], bb[:, :, wb:]


def kernel(x_prompt, x_sample, cache_a_k, cache_a_v, cache_b_k, cache_b_v, cache_mem_k, cache_mem_v,
           mem_prompt, w_in, w_br_a, w_br_b, w_br_m, w_out, w_mem_kv, rel_bias_a, sink_b,
           ffn1_gu, ffn1_down, ffn2_gu, ffn2_down, ln_g, ln_b):
    batch, seq, _ = x_prompt.shape
    streams, t, _ = x_sample.shape
    a_keep = min(A_PREV_CHUNKS * CHUNK, seq)
    b_keep = min(B_PREV_CHUNKS * CHUNK, seq)
    wa, wb = cache_a_k.shape[2], cache_b_k.shape[2]

    w_qkv = w_in.astype(BF16)
    w_gate = w_qkv[:, :, QKV_WIDTH:]
    w_br = jnp.stack([w_br_a, w_br_b, w_br_m], axis=1).astype(BF16)
    w_out_b = w_out.astype(BF16)
    w_mem_b = w_mem_kv.astype(BF16)
    f1gu, f1d = ffn1_gu.astype(BF16), ffn1_down.astype(BF16)
    f2gu, f2d = ffn2_gu.astype(BF16), ffn2_down.astype(BF16)
    lng = ln_g.reshape(DEPTH * 3, 1, D_MODEL)
    lnb = ln_b.reshape(DEPTH * 3, 1, D_MODEL)

    cak, cav = cache_a_k, cache_a_v
    cbk = cache_b_k.reshape(DEPTH, streams, wb, B_KV_WIDTH)
    cbv = cache_b_v.reshape(DEPTH, streams, wb, B_KV_WIDTH)
    cmk = cache_mem_k.reshape(DEPTH, streams, N_MEM, M_WIDTH)
    cmv = cache_mem_v.reshape(DEPTH, streams, N_MEM, M_WIDTH)

    xp = x_prompt.reshape(batch * seq, D_MODEL)
    xs = x_sample.reshape(streams * t, D_MODEL)
    xpb, xsb = xp.astype(BF16), xs.astype(BF16)
    memb = mem_prompt.reshape(batch * N_MEM, D_MODEL).astype(BF16)

    tm_p, ch_p = 512, 2
    tm_s = streams * t
    ffn_tiles_p = (2048, 8, 256, 2)
    ffn_tiles_s = (tm_s, 1, tm_s, 1)
    n_small_qkv = QKV_WIDTH // TN_SMALL
    n_small_mem = 2 * M_WIDTH // TN_SMALL
    kv_first, kv_b = COL_KA // TN_SMALL, COL_KB // TN_SMALL
    n_kv_a = 2 * A_WIDTH // TN_SMALL

    akp, avp, bkp, bvp, mkp, mvp = [], [], [], [], [], []
    aks, avs, bks, bvs = [], [], [], []
    for l in range(DEPTH):
        bias_a, bias_b = _prompt_biases(rel_bias_a[l])
        s_biases = _sample_biases(rel_bias_a[l], wa, wb, t)
        sink = sink_b[l].astype(F32) * F32(HEAD_DIM ** 0.5)

        xp, xpb = _ffn(xp, xpb, f1gu, f1d, lng, lnb, l, 0, *ffn_tiles_p)
        p = _proj(xpb, w_qkv, l, TN_QKV, lambda j: j, QKV_WIDTH // TN_QKV, 1024, 4, BF16)
        tail = xpb.reshape(batch, seq, D_MODEL)[:, seq - a_keep:].reshape(batch * a_keep, D_MODEL)
        kv = _proj(tail, w_qkv, l, TN_SMALL, lambda j: jnp.where(j < n_kv_a, j + kv_first, kv_b),
                   n_kv_a + 1, 512, 1, F32).reshape(batch, a_keep, -1)
        mkv = _proj(memb, w_mem_b, l, TN_SMALL, lambda j: j, n_small_mem, batch * N_MEM, 1, F32)
        mk, mv = mkv[:, :M_WIDTH], mkv[:, M_WIDTH:]
        att = _attn_prompt(p, mk.astype(BF16), mv.astype(BF16), bias_a, bias_b, sink, batch, seq)
        h = _merge(xpb, att, w_gate, w_br, l, tm_p, ch_p)
        xp, xpb = _outproj(xp, h, w_out_b, lng, lnb, l, tm_p, 4)
        xp, xpb = _ffn(xp, xpb, f2gu, f2d, lng, lnb, l, 2, *ffn_tiles_p)

        akp.append(kv[:, :, :A_WIDTH].reshape(batch, a_keep, A_HEADS, HEAD_DIM))
        avp.append(kv[:, :, A_WIDTH:2 * A_WIDTH].reshape(batch, a_keep, A_HEADS, HEAD_DIM))
        kvb = kv[:, a_keep - b_keep:, 2 * A_WIDTH:]
        bkp.append(kvb[:, :, :B_KV_WIDTH].reshape(batch, b_keep, B_KV_HEADS, HEAD_DIM))
        bvp.append(kvb[:, :, B_KV_WIDTH:].reshape(batch, b_keep, B_KV_HEADS, HEAD_DIM))
        mkp.append(mk.reshape(batch, N_MEM, M_HEADS, M_HEAD_DIM))
        mvp.append(mv.reshape(batch, N_MEM, M_HEADS, M_HEAD_DIM))

        xs, xsb = _ffn(xs, xsb, f1gu, f1d, lng, lnb, l, 0, *ffn_tiles_s)
        ps = _proj(xsb, w_qkv, l, TN_SMALL, lambda j: j, n_small_qkv, tm_s, 1, F32)
        att_s = _attn_sample(ps, cak, cav, cbk, cbv, cmk, cmv, s_biases, sink, l, streams, t)
        hs = _merge(xsb, att_s, w_gate, w_br, l, tm_s, 1)
        xs, xsb = _outproj(xs, hs, w_out_b, lng, lnb, l, tm_s, 1)
        xs, xsb = _ffn(xs, xsb, f2gu, f2d, lng, lnb, l, 2, *ffn_tiles_s)

        ps3 = ps.reshape(streams, t, QKV_WIDTH)
        aks.append(ps3[:, :, COL_KA:COL_KA + A_WIDTH].reshape(streams, t, A_HEADS, HEAD_DIM))
        avs.append(ps3[:, :, COL_VA:COL_VA + A_WIDTH].reshape(streams, t, A_HEADS, HEAD_DIM))
        bks.append(ps3[:, :, COL_KB:COL_KB + B_KV_WIDTH].reshape(streams, t, B_KV_HEADS, HEAD_DIM))
        bvs.append(ps3[:, :, COL_VB:COL_VB + B_KV_WIDTH].reshape(streams, t, B_KV_HEADS, HEAD_DIM))

    return (xp.reshape(batch, seq, D_MODEL), xs.reshape(streams, t, D_MODEL),
            jnp.stack(akp), jnp.stack(avp), jnp.stack(bkp), jnp.stack(bvp), jnp.stack(mkp), jnp.stack(mvp),
            jnp.stack(aks), jnp.stack(avs), jnp.stack(bks), jnp.stack(bvs))
```

```python
import functools

import numpy as np
import jax
import jax.numpy as jnp
from jax import lax
from jax.experimental import pallas as pl
from jax.experimental.pallas import tpu as pltpu

F32 = jnp.float32
BF16 = jnp.bfloat16

D_MODEL = 2048
DEPTH = 4
CHUNK = 64
HEAD_DIM = 128
A_HEADS = 8
A_PREV_CHUNKS = 8
A_REL_MAX = 128
B_HEADS = 8
B_KV_HEADS = 2
B_PREV_CHUNKS = 2
M_HEADS = 4
M_HEAD_DIM = 256
N_MEM = 256
D_FF = 5632

A_WIDTH = A_HEADS * HEAD_DIM
B_Q_WIDTH = B_HEADS * HEAD_DIM
B_KV_WIDTH = B_KV_HEADS * HEAD_DIM
M_WIDTH = M_HEADS * M_HEAD_DIM
QKV_WIDTH = 3 * A_WIDTH + B_Q_WIDTH + 2 * B_KV_WIDTH + M_WIDTH
COL_QA, COL_KA, COL_VA = 0, A_WIDTH, 2 * A_WIDTH
COL_QB = 3 * A_WIDTH
COL_KB = COL_QB + B_Q_WIDTH
COL_VB = COL_KB + B_KV_WIDTH
COL_QM = COL_VB + B_KV_WIDTH
ATT_WIDTH = A_WIDTH + B_Q_WIDTH + M_WIDTH

DEEPNORM_ALPHA = (2 * DEPTH) ** 0.25
LN_EPS = 1e-5
NEG_INF = -1e30

MIB = 1024 * 1024
QB = 4 * CHUNK
N_BIAS_VARIANTS = A_PREV_CHUNKS * CHUNK // QB + 1
MXU_WIDTH = 256
HEAD_GROUP = 4
TF = 512
TN_QKV = QKV_WIDTH // 2
TN_SMALL = 512


def _params(sem, vmem_mib):
    return pltpu.CompilerParams(dimension_semantics=sem, vmem_limit_bytes=vmem_mib * MIB)


def _layernorm(z, g, b):
    mu = jnp.mean(z, axis=-1, keepdims=True)
    zc = z - mu
    var = jnp.mean(zc * zc, axis=-1, keepdims=True)
    return zc * lax.rsqrt(var + LN_EPS) * g + b


def _row_chunks(tm, n_chunks):
    rc = tm // n_chunks
    return [slice(r * rc, (r + 1) * rc) for r in range(n_chunks)]


def _store_norm(z, g_ref, b_ref, y_ref, yb_ref, rows):
    y = _layernorm(z, g_ref[...], b_ref[...])
    y_ref[rows, :] = y
    yb_ref[rows, :] = y.astype(BF16)


def _ffn_up_body(xb_ref, wg_ref, wu_ref, h_ref, *, n_chunks):
    for rows in _row_chunks(xb_ref.shape[0], n_chunks):
        xb = xb_ref[rows, :]
        gate = jnp.dot(xb, wg_ref[...], preferred_element_type=F32)
        up = jnp.dot(xb, wu_ref[...], preferred_element_type=F32)
        h_ref[rows, :] = (gate * jax.nn.sigmoid(gate) * up).astype(h_ref.dtype)


def _residual_norm_chunks(x_ref, lhs_ref, w_ref, g_ref, b_ref, y_ref, yb_ref, n_chunks, branch_weight):
    for rows in _row_chunks(x_ref.shape[0], n_chunks):
        d = jnp.dot(lhs_ref[rows, :], w_ref[...], preferred_element_type=F32)
        branch = d if branch_weight == 1.0 else branch_weight * d
        _store_norm(DEEPNORM_ALPHA * x_ref[rows, :] + branch, g_ref, b_ref, y_ref, yb_ref, rows)


def _ffn_down_body(x_ref, h_ref, wd_ref, g_ref, b_ref, y_ref, yb_ref, *, n_chunks):
    _residual_norm_chunks(x_ref, h_ref, wd_ref, g_ref, b_ref, y_ref, yb_ref, n_chunks, 0.5)


def _ffn(x, xb, w_gu, w_down, ln_g, ln_b, layer, ln_idx, tm_up, ch_up, tm_down, ch_down):
    rows = x.shape[0]
    nj = D_FF // TF
    ln_row = layer * 3 + ln_idx
    hidden = pl.pallas_call(
        functools.partial(_ffn_up_body, n_chunks=ch_up),
        grid=(rows // tm_up, nj),
        in_specs=[
            pl.BlockSpec((tm_up, D_MODEL), lambda i, j: (i, 0)),
            pl.BlockSpec((None, D_MODEL, TF), lambda i, j: (layer, 0, j)),
            pl.BlockSpec((None, D_MODEL, TF), lambda i, j: (layer, 0, j + nj)),
        ],
        out_specs=pl.BlockSpec((tm_up, TF), lambda i, j: (i, j)),
        out_shape=jax.ShapeDtypeStruct((rows, D_FF), BF16),
        compiler_params=_params(("parallel", "arbitrary"), 48),
        name="ffn_up",
    )(xb, w_gu, w_gu)
    row_spec = pl.BlockSpec((tm_down, D_MODEL), lambda i: (i, 0))
    ln_spec = pl.BlockSpec((None, 1, D_MODEL), lambda i: (ln_row, 0, 0))
    return pl.pallas_call(
        functools.partial(_ffn_down_body, n_chunks=ch_down),
        grid=(rows // tm_down,),
        in_specs=[
            row_spec,
            pl.BlockSpec((tm_down, D_FF), lambda i: (i, 0)),
            pl.BlockSpec((None, D_FF, D_MODEL), lambda i: (layer, 0, 0), pipeline_mode=pl.Buffered(1)),
            ln_spec, ln_spec,
        ],
        out_specs=[row_spec, row_spec],
        out_shape=[jax.ShapeDtypeStruct((rows, D_MODEL), F32), jax.ShapeDtypeStruct((rows, D_MODEL), BF16)],
        compiler_params=_params(("parallel",), 52),
        name="ffn_down",
    )(x, hidden, w_down, ln_g, ln_b)


def _proj_body(xb_ref, w_ref, o_ref, *, n_chunks):
    for rows in _row_chunks(xb_ref.shape[0], n_chunks):
        o_ref[rows, :] = jnp.dot(xb_ref[rows, :], w_ref[...], preferred_element_type=F32).astype(o_ref.dtype)


def _proj(xb, w, layer, tn, col_block, n_col_blocks, tm, n_chunks, out_dtype):
    rows = xb.shape[0]
    return pl.pallas_call(
        functools.partial(_proj_body, n_chunks=n_chunks),
        grid=(rows // tm, n_col_blocks),
        in_specs=[
            pl.BlockSpec((tm, D_MODEL), lambda i, j: (i, 0)),
            pl.BlockSpec((None, D_MODEL, tn), lambda i, j: (layer, 0, col_block(j))),
        ],
        out_specs=pl.BlockSpec((tm, tn), lambda i, j: (i, j)),
        out_shape=jax.ShapeDtypeStruct((rows, n_col_blocks * tn), out_dtype),
        compiler_params=_params(("parallel", "arbitrary"), 52),
        name="proj",
    )(xb, w)


def _qk(q, k):
    return lax.dot_general(q, k, (((1,), (1,)), ((), ())), preferred_element_type=F32)


def _softmax_pv(scores, values, scale, sink=None):
    return _softmax_pv_group([(scores, values, sink)], scale)[0]


def _softmax_pv_group(heads, scale):
    c = F32(scale * np.log2(np.e))
    ms = []
    for scores, _, sink in heads:
        m = scores[0].max(axis=-1, keepdims=True)
        for s in scores[1:]:
            m = jnp.maximum(m, s.max(axis=-1, keepdims=True))
        ms.append(m if sink is None else jnp.maximum(m, sink))
    es = [[jnp.exp2((s - m) * c) for s in scores] for (scores, _, _), m in zip(heads, ms)]
    outs = []
    for (_, values, sink), m, e_blocks in zip(heads, ms, es):
        d = values[0].shape[-1]
        ones_cols = d < MXU_WIDTH
        o = None
        for e, v in zip(e_blocks, values):
            rhs = jnp.concatenate([v, jnp.ones_like(v)], axis=1) if ones_cols else v
            part = jnp.dot(e.astype(BF16), rhs, preferred_element_type=F32)
            o = part if o is None else o + part
        if ones_cols:
            o, denom = o[:, :d], o[:, d:]
        else:
            denom = e_blocks[0].sum(axis=-1, keepdims=True)
            for e in e_blocks[1:]:
                denom = denom + e.sum(axis=-1, keepdims=True)
        if sink is not None:
            denom = denom + jnp.exp2((sink - m) * c)
        outs.append(o / denom)
    return outs


def _hs(h, width=HEAD_DIM):
    return slice(h * width, (h + 1) * width)


def _attn_prompt_body(sink_ref, cur_ref, ka2_ref, ka1_ref, va2_ref, va1_ref, kbp_ref, vbp_ref,
                      mk_ref, mv_ref, bias_a_ref, bias_b_ref, o_ref):
    scale = HEAD_DIM ** -0.5
    rep = B_HEADS // B_KV_HEADS
    half = QB // 2

    def cur(c0, h, width=HEAD_DIM):
        return cur_ref[:, c0 + h * width:c0 + (h + 1) * width]

    def head_a(h):
        q = cur(COL_QA, h)
        t2 = _qk(q, ka2_ref[:, _hs(h)]) + bias_a_ref[h, :, 0:QB]
        t1 = _qk(q, ka1_ref[:, _hs(h)]) + bias_a_ref[h, :, QB:2 * QB]
        t0 = _qk(q, cur(COL_KA, h)) + bias_a_ref[h, :, 2 * QB:3 * QB]
        return [t2, t1, t0], [va2_ref[:, _hs(h)], va1_ref[:, _hs(h)], cur(COL_VA, h)], None

    def head_b(h):
        g = h // rep
        q = cur(COL_QB, h)
        t1 = _qk(q, kbp_ref[:, _hs(g)]) + bias_b_ref[h, :, 0:half]
        t0 = _qk(q, cur(COL_KB, g)) + bias_b_ref[h, :, half:half + QB]
        return [t1, t0], [vbp_ref[:, _hs(g)], cur(COL_VB, g)], sink_ref[h]

    def head_m(h):
        t = _qk(cur(COL_QM, h, M_HEAD_DIM), mk_ref[:, _hs(h, M_HEAD_DIM)])
        return [t], [mv_ref[:, _hs(h, M_HEAD_DIM)]], None

    branches = ((head_a, A_HEADS, HEAD_DIM, scale, 0), (head_b, B_HEADS, HEAD_DIM, scale, A_WIDTH),
                (head_m, M_HEADS, M_HEAD_DIM, M_HEAD_DIM ** -0.5, A_WIDTH + B_Q_WIDTH))
    for head, n_heads, width, sc, base in branches:
        group = min(HEAD_GROUP, n_heads)
        for h0 in range(0, n_heads, group):
            outs = _softmax_pv_group([head(h) for h in range(h0, h0 + group)], sc)
            for h, o in zip(range(h0, h0 + group), outs):
                o_ref[:, base + h * width:base + (h + 1) * width] = o.astype(o_ref.dtype)


def _attn_prompt(p, mk, mv, bias_a, bias_b, sink, batch, seq):
    nb = seq // QB
    half = QB // 2
    ka_blk, va_blk = COL_KA // A_WIDTH, COL_VA // A_WIDTH
    kb_blk, vb_blk = COL_KB // B_KV_WIDTH, COL_VB // B_KV_WIDTH

    def prev_a(col, back):
        return pl.BlockSpec((QB, A_WIDTH), lambda b, i: (b * nb + jnp.maximum(i - back, 0), col))

    def prev_b(col):
        return pl.BlockSpec((half, B_KV_WIDTH),
                            lambda b, i: (b * 2 * nb + jnp.maximum(2 * i - 1, 0), col))

    return pl.pallas_call(
        _attn_prompt_body,
        grid=(batch, nb),
        in_specs=[
            pl.BlockSpec(memory_space=pltpu.SMEM),
            pl.BlockSpec((QB, QKV_WIDTH), lambda b, i: (b * nb + i, 0)),
            prev_a(ka_blk, 2), prev_a(ka_blk, 1), prev_a(va_blk, 2), prev_a(va_blk, 1),
            prev_b(kb_blk), prev_b(vb_blk),
            pl.BlockSpec((N_MEM, M_WIDTH), lambda b, i: (b, 0)),
            pl.BlockSpec((N_MEM, M_WIDTH), lambda b, i: (b, 0)),
            pl.BlockSpec((None, A_HEADS, QB, 3 * QB), lambda b, i: (jnp.minimum(i, N_BIAS_VARIANTS - 1), 0, 0, 0)),
            pl.BlockSpec((None, B_HEADS, QB, half + QB),
                         lambda b, i: (jnp.minimum(i, N_BIAS_VARIANTS - 1), 0, 0, 0)),
        ],
        out_specs=pl.BlockSpec((QB, ATT_WIDTH), lambda b, i: (b * nb + i, 0)),
        out_shape=jax.ShapeDtypeStruct((batch * seq, ATT_WIDTH), BF16),
        compiler_params=_params(("parallel", "arbitrary"), 48),
        name="attn_prompt",
    )(sink, p, p, p, p, p, p, p, mk, mv, bias_a, bias_b)


def _attn_sample_body(sink_ref, cur_ref, cak_ref, cav_ref, cbk_ref, cbv_ref, cmk_ref, cmv_ref,
                      bias_ac_ref, bias_an_ref, bias_bc_ref, bias_bn_ref, o_ref):
    scale = HEAD_DIM ** -0.5

    def col(c0, h, width=HEAD_DIM):
        return cur_ref[:, c0 + h * width:c0 + (h + 1) * width].astype(BF16)

    for h in range(A_HEADS):
        q = col(COL_QA, h)
        ks = [cak_ref[:, h, :].astype(BF16), col(COL_KA, h)]
        vs = [cav_ref[:, h, :].astype(BF16), col(COL_VA, h)]
        tc = _qk(q, ks[0]) + bias_ac_ref[h]
        tn = _qk(q, ks[1]) + bias_an_ref[h]
        o_ref[:, _hs(h)] = _softmax_pv([tc, tn], vs, scale).astype(o_ref.dtype)

    rep = B_HEADS // B_KV_HEADS
    for h in range(B_HEADS):
        g = h // rep
        q = col(COL_QB, h)
        ks = [cbk_ref[:, _hs(g)].astype(BF16), col(COL_KB, g)]
        vs = [cbv_ref[:, _hs(g)].astype(BF16), col(COL_VB, g)]
        tc = _qk(q, ks[0]) + bias_bc_ref[h]
        tn = _qk(q, ks[1]) + bias_bn_ref[h]
        o = _softmax_pv([tc, tn], vs, scale, sink=sink_ref[h])
        o_ref[:, A_WIDTH + h * HEAD_DIM:A_WIDTH + (h + 1) * HEAD_DIM] = o.astype(o_ref.dtype)

    for h in range(M_HEADS):
        q = col(COL_QM, h, M_HEAD_DIM)
        t = _qk(q, cmk_ref[:, _hs(h, M_HEAD_DIM)].astype(BF16))
        o = _softmax_pv([t], [cmv_ref[:, _hs(h, M_HEAD_DIM)].astype(BF16)], M_HEAD_DIM ** -0.5)
        base = A_WIDTH + B_Q_WIDTH
        o_ref[:, base + h * M_HEAD_DIM:base + (h + 1) * M_HEAD_DIM] = o.astype(o_ref.dtype)


def _attn_sample(p, cak, cav, cbk, cbv, cmk, cmv, biases, sink, layer, streams, t):
    bias_ac, bias_an, bias_bc, bias_bn = biases

    def cache_spec(c):
        return pl.BlockSpec((None, None) + c.shape[2:], lambda s: (layer, s) + (0,) * (c.ndim - 2))

    def full_spec(a):
        return pl.BlockSpec(a.shape, lambda s: (0, 0, 0))

    return pl.pallas_call(
        _attn_sample_body,
        grid=(streams,),
        in_specs=[
            pl.BlockSpec(memory_space=pltpu.SMEM),
            pl.BlockSpec((t, QKV_WIDTH), lambda s: (s, 0)),
            cache_spec(cak), cache_spec(cav), cache_spec(cbk), cache_spec(cbv),
            cache_spec(cmk), cache_spec(cmv),
            full_spec(bias_ac), full_spec(bias_an), full_spec(bias_bc), full_spec(bias_bn),
        ],
        out_specs=pl.BlockSpec((t, ATT_WIDTH), lambda s: (s, 0)),
        out_shape=jax.ShapeDtypeStruct((streams * t, ATT_WIDTH), BF16),
        compiler_params=_params(("arbitrary",), 32),
        name="attn_sample",
    )(sink, p, cak, cav, cbk, cbv, cmk, cmv, bias_ac, bias_an, bias_bc, bias_bn)


def _merge_body(xb_ref, o_ref, wg_ref, wbr_ref, h_ref, acc_ref, *, n_branch, n_chunks):
    j = pl.program_id(2)

    def step(first, last):
        for rows in _row_chunks(xb_ref.shape[0], n_chunks):
            gate = jax.nn.sigmoid(jnp.dot(xb_ref[rows, :], wg_ref[...], preferred_element_type=F32))
            term = gate * jnp.dot(o_ref[rows, :], wbr_ref[...], preferred_element_type=F32)
            if last:
                h_ref[rows, :] = (acc_ref[rows, :] + term).astype(h_ref.dtype)
            elif first:
                acc_ref[rows, :] = term
            else:
                acc_ref[rows, :] += term

    pl.when(j == 0)(lambda: step(True, False))
    pl.when(jnp.logical_and(j > 0, j < n_branch - 1))(lambda: step(False, False))
    pl.when(j == n_branch - 1)(lambda: step(False, True))


def _merge(xb, att, w_gate, w_br, layer, tm, tn, n_chunks):
    rows = xb.shape[0]
    n_branch = 3
    nc = D_MODEL // tn
    return pl.pallas_call(
        functools.partial(_merge_body, n_branch=n_branch, n_chunks=n_chunks),
        grid=(rows // tm, nc, n_branch),
        in_specs=[
            pl.BlockSpec((tm, D_MODEL), lambda i, c, j: (i, 0)),
            pl.BlockSpec((tm, A_WIDTH), lambda i, c, j: (i, j)),
            pl.BlockSpec((None, D_MODEL, tn), lambda i, c, j: (layer, 0, j * nc + c)),
            pl.BlockSpec((None, None, A_WIDTH, tn), lambda i, c, j: (layer, j, 0, c)),
        ],
        out_specs=pl.BlockSpec((tm, tn), lambda i, c, j: (i, c)),
        out_shape=jax.ShapeDtypeStruct((rows, D_MODEL), BF16),
        scratch_shapes=[pltpu.VMEM((tm, tn), F32)],
        compiler_params=_params(("parallel", "arbitrary", "arbitrary"), 52),
        name="merge",
    )(xb, att, w_gate, w_br)


def _outproj_body(x_ref, h_ref, w_ref, g_ref, b_ref, y_ref, yb_ref, *, n_chunks):
    _residual_norm_chunks(x_ref, h_ref, w_ref, g_ref, b_ref, y_ref, yb_ref, n_chunks, 1.0)


def _outproj(x, h, w_out, ln_g, ln_b, layer, tm, n_chunks):
    rows = x.shape[0]
    ln_row = layer * 3 + 1
    row_spec = pl.BlockSpec((tm, D_MODEL), lambda i: (i, 0))
    return pl.pallas_call(
        functools.partial(_outproj_body, n_chunks=n_chunks),
        grid=(rows // tm,),
        in_specs=[
            row_spec, row_spec,
            pl.BlockSpec((None, D_MODEL, D_MODEL), lambda i: (layer, 0, 0)),
            pl.BlockSpec((None, 1, D_MODEL), lambda i: (ln_row, 0, 0)),
            pl.BlockSpec((None, 1, D_MODEL), lambda i: (ln_row, 0, 0)),
        ],
        out_specs=[row_spec, row_spec],
        out_shape=[jax.ShapeDtypeStruct((rows, D_MODEL), F32), jax.ShapeDtypeStruct((rows, D_MODEL), BF16)],
        compiler_params=_params(("parallel",), 52),
        name="outproj",
    )(x, h, w_out, ln_g, ln_b)


def _toeplitz(row_of_delta, n_q, n_k):
    period = n_q + n_k
    k = np.arange(period)
    w = row_of_delta(np.where(k < n_k, k, k - period))
    return jnp.tile(w, (1, n_q))[:, :n_q * (period - 1)].reshape(-1, n_q, period - 1)[:, :, :n_k]


def _rel_table_bias(table, offset, n_q, n_k):
    def row(delta):
        idx = np.clip(offset - delta, -(CHUNK - 1), A_REL_MAX) + (CHUNK - 1)
        return table.astype(F32)[:, idx]
    return _toeplitz(row, n_q, n_k)


def _alibi(offset, n_q, n_k):
    slopes = 2.0 ** (-8.0 * (jnp.arange(B_HEADS, dtype=F32) + 1.0) / B_HEADS)
    rel = offset + np.arange(n_q)[:, None] - np.arange(n_k)[None, :]
    return -slopes[:, None, None] * jnp.asarray(np.abs(rel), dtype=F32)


def _band(n_keys, n_prev):
    cq = np.arange(QB)[:, None] // CHUNK
    ck = np.arange(n_keys)[None, :] // CHUNK
    return (ck >= cq) & (ck <= cq + n_prev)


def _prompt_biases(table):
    inv = F32(HEAD_DIM ** 0.5)
    n_a, n_b = 3 * QB, QB // 2 + QB
    v = np.arange(N_BIAS_VARIANTS)[:, None, None, None]
    exists_a = np.arange(n_a)[None, None, None, :] >= (N_BIAS_VARIANTS - 1 - v) * QB
    exists_b = (np.arange(n_b)[None, None, None, :] >= QB // 2) | (v >= 1)
    bias_a = _rel_table_bias(table, A_PREV_CHUNKS * CHUNK, QB, n_a) * inv
    bias_b = _alibi(B_PREV_CHUNKS * CHUNK, QB, n_b) * inv
    bias_a = jnp.where(_band(n_a, A_PREV_CHUNKS)[None, None] & exists_a, bias_a[None], NEG_INF)
    bias_b = jnp.where(_band(n_b, B_PREV_CHUNKS)[None, None] & exists_b, bias_b[None], NEG_INF)
    return bias_a, bias_b


def _sample_biases(table, wa, wb, t):
    inv = F32(HEAD_DIM ** 0.5)
    ba = _rel_table_bias(table, wa, t, wa + t) * inv
    bb = _alibi(wb, t, wb + t) * inv
    return ba[:, :, :wa], ba[:, :, wa:], bb[:, :, :wb], bb[:, :, wb:]


def kernel(x_prompt, x_sample, cache_a_k, cache_a_v, cache_b_k, cache_b_v, cache_mem_k, cache_mem_v,
           mem_prompt, w_in, w_br_a, w_br_b, w_br_m, w_out, w_mem_kv, rel_bias_a, sink_b,
           ffn1_gu, ffn1_down, ffn2_gu, ffn2_down, ln_g, ln_b):
    batch, seq, _ = x_prompt.shape
    streams, t, _ = x_sample.shape
    a_keep = min(A_PREV_CHUNKS * CHUNK, seq)
    b_keep = min(B_PREV_CHUNKS * CHUNK, seq)
    wa, wb = cache_a_k.shape[2], cache_b_k.shape[2]

    w_qkv = w_in.astype(BF16)
    w_gate = w_qkv[:, :, QKV_WIDTH:]
    w_br = jnp.stack([w_br_a, w_br_b, w_br_m], axis=1).astype(BF16)
    w_out_b = w_out.astype(BF16)
    w_mem_b = w_mem_kv.astype(BF16)
    f1gu, f1d = ffn1_gu.astype(BF16), ffn1_down.astype(BF16)
    f2gu, f2d = ffn2_gu.astype(BF16), ffn2_down.astype(BF16)
    lng = ln_g.reshape(DEPTH * 3, 1, D_MODEL)
    lnb = ln_b.reshape(DEPTH * 3, 1, D_MODEL)

    cak, cav = cache_a_k, cache_a_v
    cbk = cache_b_k.reshape(DEPTH, streams, wb, B_KV_WIDTH)
    cbv = cache_b_v.reshape(DEPTH, streams, wb, B_KV_WIDTH)
    cmk = cache_mem_k.reshape(DEPTH, streams, N_MEM, M_WIDTH)
    cmv = cache_mem_v.reshape(DEPTH, streams, N_MEM, M_WIDTH)

    xp = x_prompt.reshape(batch * seq, D_MODEL)
    xs = x_sample.reshape(streams * t, D_MODEL)
    xpb, xsb = xp.astype(BF16), xs.astype(BF16)
    memb = mem_prompt.reshape(batch * N_MEM, D_MODEL).astype(BF16)

    tm_p, ch_p = 512, 2
    tm_s = streams * t
    ffn_tiles_p = (2048, 8, 256, 2)
    ffn_tiles_s = (tm_s, 1, tm_s, 1)
    n_small_qkv = QKV_WIDTH // TN_SMALL
    n_small_mem = 2 * M_WIDTH // TN_SMALL
    kv_first, kv_b = COL_KA // TN_SMALL, COL_KB // TN_SMALL
    n_kv_a = 2 * A_WIDTH // TN_SMALL

    akp, avp, bkp, bvp, mkp, mvp = [], [], [], [], [], []
    aks, avs, bks, bvs = [], [], [], []
    for l in range(DEPTH):
        bias_a, bias_b = _prompt_biases(rel_bias_a[l])
        s_biases = _sample_biases(rel_bias_a[l], wa, wb, t)
        sink = sink_b[l].astype(F32) * F32(HEAD_DIM ** 0.5)

        xp, xpb = _ffn(xp, xpb, f1gu, f1d, lng, lnb, l, 0, *ffn_tiles_p)
        p = _proj(xpb, w_qkv, l, TN_QKV, lambda j: j, QKV_WIDTH // TN_QKV, 1024, 4, BF16)
        tail = xpb.reshape(batch, seq, D_MODEL)[:, seq - a_keep:].reshape(batch * a_keep, D_MODEL)
        kv = _proj(tail, w_qkv, l, TN_SMALL, lambda j: jnp.where(j < n_kv_a, j + kv_first, kv_b),
                   n_kv_a + 1, 512, 1, F32).reshape(batch, a_keep, -1)
        mkv = _proj(memb, w_mem_b, l, TN_SMALL, lambda j: j, n_small_mem, batch * N_MEM, 1, F32)
        mk, mv = mkv[:, :M_WIDTH], mkv[:, M_WIDTH:]
        att = _attn_prompt(p, mk.astype(BF16), mv.astype(BF16), bias_a, bias_b, sink, batch, seq)
        h = _merge(xpb, att, w_gate, w_br, l, 1024, 1024, 4)
        xp, xpb = _outproj(xp, h, w_out_b, lng, lnb, l, tm_p, 4)
        xp, xpb = _ffn(xp, xpb, f2gu, f2d, lng, lnb, l, 2, *ffn_tiles_p)

        akp.append(kv[:, :, :A_WIDTH].reshape(batch, a_keep, A_HEADS, HEAD_DIM))
        avp.append(kv[:, :, A_WIDTH:2 * A_WIDTH].reshape(batch, a_keep, A_HEADS, HEAD_DIM))
        kvb = kv[:, a_keep - b_keep:, 2 * A_WIDTH:]
        bkp.append(kvb[:, :, :B_KV_WIDTH].reshape(batch, b_keep, B_KV_HEADS, HEAD_DIM))
        bvp.append(kvb[:, :, B_KV_WIDTH:].reshape(batch, b_keep, B_KV_HEADS, HEAD_DIM))
        mkp.append(mk.reshape(batch, N_MEM, M_HEADS, M_HEAD_DIM))
        mvp.append(mv.reshape(batch, N_MEM, M_HEADS, M_HEAD_DIM))

        xs, xsb = _ffn(xs, xsb, f1gu, f1d, lng, lnb, l, 0, *ffn_tiles_s)
        ps = _proj(xsb, w_qkv, l, TN_SMALL, lambda j: j, n_small_qkv, tm_s, 1, F32)
        att_s = _attn_sample(ps, cak, cav, cbk, cbv, cmk, cmv, s_biases, sink, l, streams, t)
        hs = _merge(xsb, att_s, w_gate, w_br, l, tm_s, D_MODEL, 1)
        xs, xsb = _outproj(xs, hs, w_out_b, lng, lnb, l, tm_s, 1)
        xs, xsb = _ffn(xs, xsb, f2gu, f2d, lng, lnb, l, 2, *ffn_tiles_s)

        ps3 = ps.reshape(streams, t, QKV_WIDTH)
        aks.append(ps3[:, :, COL_KA:COL_KA + A_WIDTH].reshape(streams, t, A_HEADS, HEAD_DIM))
        avs.append(ps3[:, :, COL_VA:COL_VA + A_WIDTH].reshape(streams, t, A_HEADS, HEAD_DIM))
        bks.append(ps3[:, :, COL_KB:COL_KB + B_KV_WIDTH].reshape(streams, t, B_KV_HEADS, HEAD_DIM))
        bvs.append(ps3[:, :, COL_VB:COL_VB + B_KV_WIDTH].reshape(streams, t, B_KV_HEADS, HEAD_DIM))

    return (xp.reshape(batch, seq, D_MODEL), xs.reshape(streams, t, D_MODEL),
            jnp.stack(akp), jnp.stack(avp), jnp.stack(bkp), jnp.stack(bvp), jnp.stack(mkp), jnp.stack(mvp),
            jnp.stack(aks), jnp.stack(avs), jnp.stack(bks), jnp.stack(bvs))
```

```python
import functools

import numpy as np
import jax
import jax.numpy as jnp
from jax import lax
from jax.experimental import pallas as pl
from jax.experimental.pallas import tpu as pltpu

F32 = jnp.float32
BF16 = jnp.bfloat16

D_MODEL = 2048
DEPTH = 4
CHUNK = 64
HEAD_DIM = 128
A_HEADS = 8
A_PREV_CHUNKS = 8
A_REL_MAX = 128
B_HEADS = 8
B_KV_HEADS = 2
B_PREV_CHUNKS = 2
M_HEADS = 4
M_HEAD_DIM = 256
N_MEM = 256
D_FF = 5632

A_WIDTH = A_HEADS * HEAD_DIM
B_Q_WIDTH = B_HEADS * HEAD_DIM
B_KV_WIDTH = B_KV_HEADS * HEAD_DIM
M_WIDTH = M_HEADS * M_HEAD_DIM
QKV_WIDTH = 3 * A_WIDTH + B_Q_WIDTH + 2 * B_KV_WIDTH + M_WIDTH
COL_QA, COL_KA, COL_VA = 0, A_WIDTH, 2 * A_WIDTH
COL_QB = 3 * A_WIDTH
COL_KB = COL_QB + B_Q_WIDTH
COL_VB = COL_KB + B_KV_WIDTH
COL_QM = COL_VB + B_KV_WIDTH
ATT_WIDTH = A_WIDTH + B_Q_WIDTH + M_WIDTH

DEEPNORM_ALPHA = (2 * DEPTH) ** 0.25
LN_EPS = 1e-5
NEG_INF = -1e30

MIB = 1024 * 1024
QB = 4 * CHUNK
N_BIAS_VARIANTS = A_PREV_CHUNKS * CHUNK // QB + 1
MXU_WIDTH = 256
HEAD_GROUP = 4
TF = 512
TN_QKV = QKV_WIDTH // 2
TN_SMALL = 512


def _params(sem, vmem_mib):
    return pltpu.CompilerParams(dimension_semantics=sem, vmem_limit_bytes=vmem_mib * MIB)


def _layernorm(z, g, b):
    mu = jnp.mean(z, axis=-1, keepdims=True)
    zc = z - mu
    var = jnp.mean(zc * zc, axis=-1, keepdims=True)
    return zc * lax.rsqrt(var + LN_EPS) * g + b


def _row_chunks(tm, n_chunks):
    rc = tm // n_chunks
    return [slice(r * rc, (r + 1) * rc) for r in range(n_chunks)]


def _store_norm(z, g_ref, b_ref, y_ref, yb_ref, rows):
    y = _layernorm(z, g_ref[...], b_ref[...])
    y_ref[rows, :] = y
    yb_ref[rows, :] = y.astype(BF16)


def _ffn_up_body(xb_ref, wg_ref, wu_ref, h_ref, *, n_chunks):
    for rows in _row_chunks(xb_ref.shape[0], n_chunks):
        xb = xb_ref[rows, :]
        gate = jnp.dot(xb, wg_ref[...], preferred_element_type=F32)
        up = jnp.dot(xb, wu_ref[...], preferred_element_type=F32)
        h_ref[rows, :] = (gate * jax.nn.sigmoid(gate) * up).astype(h_ref.dtype)


def _residual_norm_chunks(x_ref, lhs_ref, w_ref, g_ref, b_ref, y_ref, yb_ref, n_chunks, branch_weight):
    for rows in _row_chunks(x_ref.shape[0], n_chunks):
        d = jnp.dot(lhs_ref[rows, :], w_ref[...], preferred_element_type=F32)
        branch = d if branch_weight == 1.0 else branch_weight * d
        _store_norm(DEEPNORM_ALPHA * x_ref[rows, :] + branch, g_ref, b_ref, y_ref, yb_ref, rows)


def _ffn_down_body(x_ref, h_ref, wd_ref, g_ref, b_ref, y_ref, yb_ref, *, n_chunks):
    _residual_norm_chunks(x_ref, h_ref, wd_ref, g_ref, b_ref, y_ref, yb_ref, n_chunks, 0.5)


def _ffn(x, xb, w_gu, w_down, ln_g, ln_b, layer, ln_idx, tm_up, ch_up, tm_down, ch_down):
    rows = x.shape[0]
    nj = D_FF // TF
    ln_row = layer * 3 + ln_idx
    hidden = pl.pallas_call(
        functools.partial(_ffn_up_body, n_chunks=ch_up),
        grid=(rows // tm_up, nj),
        in_specs=[
            pl.BlockSpec((tm_up, D_MODEL), lambda i, j: (i, 0)),
            pl.BlockSpec((None, D_MODEL, TF), lambda i, j: (layer, 0, j)),
            pl.BlockSpec((None, D_MODEL, TF), lambda i, j: (layer, 0, j + nj)),
        ],
        out_specs=pl.BlockSpec((tm_up, TF), lambda i, j: (i, j)),
        out_shape=jax.ShapeDtypeStruct((rows, D_FF), BF16),
        compiler_params=_params(("parallel", "arbitrary"), 48),
        name="ffn_up",
    )(xb, w_gu, w_gu)
    row_spec = pl.BlockSpec((tm_down, D_MODEL), lambda i: (i, 0))
    ln_spec = pl.BlockSpec((None, 1, D_MODEL), lambda i: (ln_row, 0, 0))
    return pl.pallas_call(
        functools.partial(_ffn_down_body, n_chunks=ch_down),
        grid=(rows // tm_down,),
        in_specs=[
            row_spec,
            pl.BlockSpec((tm_down, D_FF), lambda i: (i, 0)),
            pl.BlockSpec((None, D_FF, D_MODEL), lambda i: (layer, 0, 0), pipeline_mode=pl.Buffered(1)),
            ln_spec, ln_spec,
        ],
        out_specs=[row_spec, row_spec],
        out_shape=[jax.ShapeDtypeStruct((rows, D_MODEL), F32), jax.ShapeDtypeStruct((rows, D_MODEL), BF16)],
        compiler_params=_params(("parallel",), 52),
        name="ffn_down",
    )(x, hidden, w_down, ln_g, ln_b)


def _proj_body(xb_ref, w_ref, o_ref, *, n_chunks):
    for rows in _row_chunks(xb_ref.shape[0], n_chunks):
        o_ref[rows, :] = jnp.dot(xb_ref[rows, :], w_ref[...], preferred_element_type=F32).astype(o_ref.dtype)


def _proj(xb, w, layer, tn, col_block, n_col_blocks, tm, n_chunks, out_dtype):
    rows = xb.shape[0]
    return pl.pallas_call(
        functools.partial(_proj_body, n_chunks=n_chunks),
        grid=(rows // tm, n_col_blocks),
        in_specs=[
            pl.BlockSpec((tm, D_MODEL), lambda i, j: (i, 0)),
            pl.BlockSpec((None, D_MODEL, tn), lambda i, j: (layer, 0, col_block(j))),
        ],
        out_specs=pl.BlockSpec((tm, tn), lambda i, j: (i, j)),
        out_shape=jax.ShapeDtypeStruct((rows, n_col_blocks * tn), out_dtype),
        compiler_params=_params(("parallel", "arbitrary"), 52),
        name="proj",
    )(xb, w)


def _qk(q, k):
    return lax.dot_general(q, k, (((1,), (1,)), ((), ())), preferred_element_type=F32)


def _softmax_pv(scores, values, scale, sink=None):
    return _softmax_pv_group([(scores, values, sink)], scale)[0]


def _softmax_pv_group(heads, scale):
    c = F32(scale * np.log2(np.e))
    ms = []
    for scores, _, sink in heads:
        m = scores[0].max(axis=-1, keepdims=True)
        for s in scores[1:]:
            m = jnp.maximum(m, s.max(axis=-1, keepdims=True))
        ms.append(m if sink is None else jnp.maximum(m, sink))
    es = [[jnp.exp2((s - m) * c) for s in scores] for (scores, _, _), m in zip(heads, ms)]
    outs = []
    for (_, values, sink), m, e_blocks in zip(heads, ms, es):
        d = values[0].shape[-1]
        ones_cols = d < MXU_WIDTH
        o = None
        for e, v in zip(e_blocks, values):
            rhs = jnp.concatenate([v, jnp.ones_like(v)], axis=1) if ones_cols else v
            part = jnp.dot(e.astype(BF16), rhs, preferred_element_type=F32)
            o = part if o is None else o + part
        if ones_cols:
            o, denom = o[:, :d], o[:, d:]
        else:
            denom = e_blocks[0].sum(axis=-1, keepdims=True)
            for e in e_blocks[1:]:
                denom = denom + e.sum(axis=-1, keepdims=True)
        if sink is not None:
            denom = denom + jnp.exp2((sink - m) * c)
        outs.append(o / denom)
    return outs


def _hs(h, width=HEAD_DIM):
    return slice(h * width, (h + 1) * width)


def _attn_prompt_body(sink_ref, cur_ref, ka2_ref, ka1_ref, va2_ref, va1_ref, kbp_ref, vbp_ref,
                      mk_ref, mv_ref, bias_a_ref, bias_b_ref, o_ref):
    scale = HEAD_DIM ** -0.5
    rep = B_HEADS // B_KV_HEADS
    half = QB // 2

    def cur(c0, h, width=HEAD_DIM):
        return cur_ref[:, c0 + h * width:c0 + (h + 1) * width]

    def head_a(h):
        q = cur(COL_QA, h)
        t2 = _qk(q, ka2_ref[:, _hs(h)]) + bias_a_ref[h, :, 0:QB]
        t1 = _qk(q, ka1_ref[:, _hs(h)]) + bias_a_ref[h, :, QB:2 * QB]
        t0 = _qk(q, cur(COL_KA, h)) + bias_a_ref[h, :, 2 * QB:3 * QB]
        return [t2, t1, t0], [va2_ref[:, _hs(h)], va1_ref[:, _hs(h)], cur(COL_VA, h)], None

    def head_b(h):
        g = h // rep
        q = cur(COL_QB, h)
        t1 = _qk(q, kbp_ref[:, _hs(g)]) + bias_b_ref[h, :, 0:half]
        t0 = _qk(q, cur(COL_KB, g)) + bias_b_ref[h, :, half:half + QB]
        return [t1, t0], [vbp_ref[:, _hs(g)], cur(COL_VB, g)], sink_ref[h]

    def head_m(h):
        t = _qk(cur(COL_QM, h, M_HEAD_DIM), mk_ref[:, _hs(h, M_HEAD_DIM)])
        return [t], [mv_ref[:, _hs(h, M_HEAD_DIM)]], None

    per_mixer = HEAD_GROUP // 2
    for h0 in range(0, A_HEADS, per_mixer):
        hs = list(range(h0, h0 + per_mixer))
        outs = _softmax_pv_group([head_a(h) for h in hs] + [head_b(h) for h in hs], scale)
        for k, h in enumerate(hs):
            o_ref[:, _hs(h)] = outs[k].astype(o_ref.dtype)
            o_ref[:, A_WIDTH + h * HEAD_DIM:A_WIDTH + (h + 1) * HEAD_DIM] = outs[per_mixer + k].astype(o_ref.dtype)
    base = A_WIDTH + B_Q_WIDTH
    outs = _softmax_pv_group([head_m(h) for h in range(M_HEADS)], M_HEAD_DIM ** -0.5)
    for h, o in enumerate(outs):
        o_ref[:, base + h * M_HEAD_DIM:base + (h + 1) * M_HEAD_DIM] = o.astype(o_ref.dtype)


def _attn_prompt(p, mk, mv, bias_a, bias_b, sink, batch, seq):
    nb = seq // QB
    half = QB // 2
    ka_blk, va_blk = COL_KA // A_WIDTH, COL_VA // A_WIDTH
    kb_blk, vb_blk = COL_KB // B_KV_WIDTH, COL_VB // B_KV_WIDTH

    def prev_a(col, back):
        return pl.BlockSpec((QB, A_WIDTH), lambda b, i: (b * nb + jnp.maximum(i - back, 0), col))

    def prev_b(col):
        return pl.BlockSpec((half, B_KV_WIDTH),
                            lambda b, i: (b * 2 * nb + jnp.maximum(2 * i - 1, 0), col))

    return pl.pallas_call(
        _attn_prompt_body,
        grid=(batch, nb),
        in_specs=[
            pl.BlockSpec(memory_space=pltpu.SMEM),
            pl.BlockSpec((QB, QKV_WIDTH), lambda b, i: (b * nb + i, 0)),
            prev_a(ka_blk, 2), prev_a(ka_blk, 1), prev_a(va_blk, 2), prev_a(va_blk, 1),
            prev_b(kb_blk), prev_b(vb_blk),
            pl.BlockSpec((N_MEM, M_WIDTH), lambda b, i: (b, 0)),
            pl.BlockSpec((N_MEM, M_WIDTH), lambda b, i: (b, 0)),
            pl.BlockSpec((None, A_HEADS, QB, 3 * QB), lambda b, i: (jnp.minimum(i, N_BIAS_VARIANTS - 1), 0, 0, 0)),
            pl.BlockSpec((None, B_HEADS, QB, half + QB),
                         lambda b, i: (jnp.minimum(i, N_BIAS_VARIANTS - 1), 0, 0, 0)),
        ],
        out_specs=pl.BlockSpec((QB, ATT_WIDTH), lambda b, i: (b * nb + i, 0)),
        out_shape=jax.ShapeDtypeStruct((batch * seq, ATT_WIDTH), BF16),
        compiler_params=_params(("parallel", "arbitrary"), 48),
        name="attn_prompt",
    )(sink, p, p, p, p, p, p, p, mk, mv, bias_a, bias_b)


def _attn_sample_body(sink_ref, cur_ref, cak_ref, cav_ref, cbk_ref, cbv_ref, cmk_ref, cmv_ref,
                      bias_ac_ref, bias_an_ref, bias_bc_ref, bias_bn_ref, o_ref):
    scale = HEAD_DIM ** -0.5

    def col(c0, h, width=HEAD_DIM):
        return cur_ref[:, c0 + h * width:c0 + (h + 1) * width].astype(BF16)

    for h in range(A_HEADS):
        q = col(COL_QA, h)
        ks = [cak_ref[:, h, :].astype(BF16), col(COL_KA, h)]
        vs = [cav_ref[:, h, :].astype(BF16), col(COL_VA, h)]
        tc = _qk(q, ks[0]) + bias_ac_ref[h]
        tn = _qk(q, ks[1]) + bias_an_ref[h]
        o_ref[:, _hs(h)] = _softmax_pv([tc, tn], vs, scale).astype(o_ref.dtype)

    rep = B_HEADS // B_KV_HEADS
    for h in range(B_HEADS):
        g = h // rep
        q = col(COL_QB, h)
        ks = [cbk_ref[:, _hs(g)].astype(BF16), col(COL_KB, g)]
        vs = [cbv_ref[:, _hs(g)].astype(BF16), col(COL_VB, g)]
        tc = _qk(q, ks[0]) + bias_bc_ref[h]
        tn = _qk(q, ks[1]) + bias_bn_ref[h]
        o = _softmax_pv([tc, tn], vs, scale, sink=sink_ref[h])
        o_ref[:, A_WIDTH + h * HEAD_DIM:A_WIDTH + (h + 1) * HEAD_DIM] = o.astype(o_ref.dtype)

    for h in range(M_HEADS):
        q = col(COL_QM, h, M_HEAD_DIM)
        t = _qk(q, cmk_ref[:, _hs(h, M_HEAD_DIM)].astype(BF16))
        o = _softmax_pv([t], [cmv_ref[:, _hs(h, M_HEAD_DIM)].astype(BF16)], M_HEAD_DIM ** -0.5)
        base = A_WIDTH + B_Q_WIDTH
        o_ref[:, base + h * M_HEAD_DIM:base + (h + 1) * M_HEAD_DIM] = o.astype(o_ref.dtype)


def _attn_sample(p, cak, cav, cbk, cbv, cmk, cmv, biases, sink, layer, streams, t):
    bias_ac, bias_an, bias_bc, bias_bn = biases

    def cache_spec(c):
        return pl.BlockSpec((None, None) + c.shape[2:], lambda s: (layer, s) + (0,) * (c.ndim - 2))

    def full_spec(a):
        return pl.BlockSpec(a.shape, lambda s: (0, 0, 0))

    return pl.pallas_call(
        _attn_sample_body,
        grid=(streams,),
        in_specs=[
            pl.BlockSpec(memory_space=pltpu.SMEM),
            pl.BlockSpec((t, QKV_WIDTH), lambda s: (s, 0)),
            cache_spec(cak), cache_spec(cav), cache_spec(cbk), cache_spec(cbv),
            cache_spec(cmk), cache_spec(cmv),
            full_spec(bias_ac), full_spec(bias_an), full_spec(bias_bc), full_spec(bias_bn),
        ],
        out_specs=pl.BlockSpec((t, ATT_WIDTH), lambda s: (s, 0)),
        out_shape=jax.ShapeDtypeStruct((streams * t, ATT_WIDTH), BF16),
        compiler_params=_params(("arbitrary",), 32),
        name="attn_sample",
    )(sink, p, cak, cav, cbk, cbv, cmk, cmv, bias_ac, bias_an, bias_bc, bias_bn)


def _merge_body(xb_ref, o_ref, wg_ref, wbr_ref, h_ref, acc_ref, *, n_branch, n_chunks):
    j = pl.program_id(2)

    def step(first, last):
        for rows in _row_chunks(xb_ref.shape[0], n_chunks):
            gate = jax.nn.sigmoid(jnp.dot(xb_ref[rows, :], wg_ref[...], preferred_element_type=F32))
            term = gate * jnp.dot(o_ref[rows, :], wbr_ref[...], preferred_element_type=F32)
            if last:
                h_ref[rows, :] = (acc_ref[rows, :] + term).astype(h_ref.dtype)
            elif first:
                acc_ref[rows, :] = term
            else:
                acc_ref[rows, :] += term

    pl.when(j == 0)(lambda: step(True, False))
    pl.when(jnp.logical_and(j > 0, j < n_branch - 1))(lambda: step(False, False))
    pl.when(j == n_branch - 1)(lambda: step(False, True))


def _merge(xb, att, w_gate, w_br, layer, tm, tn, n_chunks):
    rows = xb.shape[0]
    n_branch = 3
    nc = D_MODEL // tn
    return pl.pallas_call(
        functools.partial(_merge_body, n_branch=n_branch, n_chunks=n_chunks),
        grid=(rows // tm, nc, n_branch),
        in_specs=[
            pl.BlockSpec((tm, D_MODEL), lambda i, c, j: (i, 0)),
            pl.BlockSpec((tm, A_WIDTH), lambda i, c, j: (i, j)),
            pl.BlockSpec((None, D_MODEL, tn), lambda i, c, j: (layer, 0, j * nc + c)),
            pl.BlockSpec((None, None, A_WIDTH, tn), lambda i, c, j: (layer, j, 0, c)),
        ],
        out_specs=pl.BlockSpec((tm, tn), lambda i, c, j: (i, c)),
        out_shape=jax.ShapeDtypeStruct((rows, D_MODEL), BF16),
        scratch_shapes=[pltpu.VMEM((tm, tn), F32)],
        compiler_params=_params(("parallel", "arbitrary", "arbitrary"), 52),
        name="merge",
    )(xb, att, w_gate, w_br)


def _outproj_body(x_ref, h_ref, w_ref, g_ref, b_ref, y_ref, yb_ref, *, n_chunks):
    _residual_norm_chunks(x_ref, h_ref, w_ref, g_ref, b_ref, y_ref, yb_ref, n_chunks, 1.0)


def _outproj(x, h, w_out, ln_g, ln_b, layer, tm, n_chunks):
    rows = x.shape[0]
    ln_row = layer * 3 + 1
    row_spec = pl.BlockSpec((tm, D_MODEL), lambda i: (i, 0))
    return pl.pallas_call(
        functools.partial(_outproj_body, n_chunks=n_chunks),
        grid=(rows // tm,),
        in_specs=[
            row_spec, row_spec,
            pl.BlockSpec((None, D_MODEL, D_MODEL), lambda i: (layer, 0, 0)),
            pl.BlockSpec((None, 1, D_MODEL), lambda i: (ln_row, 0, 0)),
            pl.BlockSpec((None, 1, D_MODEL), lambda i: (ln_row, 0, 0)),
        ],
        out_specs=[row_spec, row_spec],
        out_shape=[jax.ShapeDtypeStruct((rows, D_MODEL), F32), jax.ShapeDtypeStruct((rows, D_MODEL), BF16)],
        compiler_params=_params(("parallel",), 52),
        name="outproj",
    )(x, h, w_out, ln_g, ln_b)


def _toeplitz(row_of_delta, n_q, n_k):
    period = n_q + n_k
    k = np.arange(period)
    w = row_of_delta(np.where(k < n_k, k, k - period))
    return jnp.tile(w, (1, n_q))[:, :n_q * (period - 1)].reshape(-1, n_q, period - 1)[:, :, :n_k]


def _rel_table_bias(table, offset, n_q, n_k):
    def row(delta):
        idx = np.clip(offset - delta, -(CHUNK - 1), A_REL_MAX) + (CHUNK - 1)
        return table.astype(F32)[:, idx]
    return _toeplitz(row, n_q, n_k)


def _alibi(offset, n_q, n_k):
    slopes = 2.0 ** (-8.0 * (jnp.arange(B_HEADS, dtype=F32) + 1.0) / B_HEADS)
    rel = offset + np.arange(n_q)[:, None] - np.arange(n_k)[None, :]
    return -slopes[:, None, None] * jnp.asarray(np.abs(rel), dtype=F32)


def _band(n_keys, n_prev):
    cq = np.arange(QB)[:, None] // CHUNK
    ck = np.arange(n_keys)[None, :] // CHUNK
    return (ck >= cq) & (ck <= cq + n_prev)


def _prompt_biases(table):
    inv = F32(HEAD_DIM ** 0.5)
    n_a, n_b = 3 * QB, QB // 2 + QB
    v = np.arange(N_BIAS_VARIANTS)[:, None, None, None]
    exists_a = np.arange(n_a)[None, None, None, :] >= (N_BIAS_VARIANTS - 1 - v) * QB
    exists_b = (np.arange(n_b)[None, None, None, :] >= QB // 2) | (v >= 1)
    bias_a = _rel_table_bias(table, A_PREV_CHUNKS * CHUNK, QB, n_a) * inv
    bias_b = _alibi(B_PREV_CHUNKS * CHUNK, QB, n_b) * inv
    bias_a = jnp.where(_band(n_a, A_PREV_CHUNKS)[None, None] & exists_a, bias_a[None], NEG_INF)
    bias_b = jnp.where(_band(n_b, B_PREV_CHUNKS)[None, None] & exists_b, bias_b[None], NEG_INF)
    return bias_a, bias_b


def _sample_biases(table, wa, wb, t):
    inv = F32(HEAD_DIM ** 0.5)
    ba = _rel_table_bias(table, wa, t, wa + t) * inv
    bb = _alibi(wb, t, wb + t) * inv
    return ba[:, :, :wa], ba[:, :, wa:], bb[:, :, :wb], bb[:, :, wb:]


def kernel(x_prompt, x_sample, cache_a_k, cache_a_v, cache_b_k, cache_b_v, cache_mem_k, cache_mem_v,
           mem_prompt, w_in, w_br_a, w_br_b, w_br_m, w_out, w_mem_kv, rel_bias_a, sink_b,
           ffn1_gu, ffn1_down, ffn2_gu, ffn2_down, ln_g, ln_b):
    batch, seq, _ = x_prompt.shape
    streams, t, _ = x_sample.shape
    a_keep = min(A_PREV_CHUNKS * CHUNK, seq)
    b_keep = min(B_PREV_CHUNKS * CHUNK, seq)
    wa, wb = cache_a_k.shape[2], cache_b_k.shape[2]

    w_qkv = w_in.astype(BF16)
    w_gate = w_qkv[:, :, QKV_WIDTH:]
    w_br = jnp.stack([w_br_a, w_br_b, w_br_m], axis=1).astype(BF16)
    w_out_b = w_out.astype(BF16)
    w_mem_b = w_mem_kv.astype(BF16)
    f1gu, f1d = ffn1_gu.astype(BF16), ffn1_down.astype(BF16)
    f2gu, f2d = ffn2_gu.astype(BF16), ffn2_down.astype(BF16)
    lng = ln_g.reshape(DEPTH * 3, 1, D_MODEL)
    lnb = ln_b.reshape(DEPTH * 3, 1, D_MODEL)

    cak, cav = cache_a_k, cache_a_v
    cbk = cache_b_k.reshape(DEPTH, streams, wb, B_KV_WIDTH)
    cbv = cache_b_v.reshape(DEPTH, streams, wb, B_KV_WIDTH)
    cmk = cache_mem_k.reshape(DEPTH, streams, N_MEM, M_WIDTH)
    cmv = cache_mem_v.reshape(DEPTH, streams, N_MEM, M_WIDTH)

    xp = x_prompt.reshape(batch * seq, D_MODEL)
    xs = x_sample.reshape(streams * t, D_MODEL)
    xpb, xsb = xp.astype(BF16), xs.astype(BF16)
    memb = mem_prompt.reshape(batch * N_MEM, D_MODEL).astype(BF16)

    tm_p, ch_p = 512, 2
    tm_s = streams * t
    ffn_tiles_p = (2048, 8, 256, 2)
    ffn_tiles_s = (tm_s, 1, tm_s, 1)
    n_small_qkv = QKV_WIDTH // TN_SMALL
    n_small_mem = 2 * M_WIDTH // TN_SMALL
    kv_first, kv_b = COL_KA // TN_SMALL, COL_KB // TN_SMALL
    n_kv_a = 2 * A_WIDTH // TN_SMALL

    akp, avp, bkp, bvp, mkp, mvp = [], [], [], [], [], []
    aks, avs, bks, bvs = [], [], [], []
    for l in range(DEPTH):
        bias_a, bias_b = _prompt_biases(rel_bias_a[l])
        s_biases = _sample_biases(rel_bias_a[l], wa, wb, t)
        sink = sink_b[l].astype(F32) * F32(HEAD_DIM ** 0.5)

        xp, xpb = _ffn(xp, xpb, f1gu, f1d, lng, lnb, l, 0, *ffn_tiles_p)
        p = _proj(xpb, w_qkv, l, TN_QKV, lambda j: j, QKV_WIDTH // TN_QKV, 1024, 4, BF16)
        tail = xpb.reshape(batch, seq, D_MODEL)[:, seq - a_keep:].reshape(batch * a_keep, D_MODEL)
        kv = _proj(tail, w_qkv, l, TN_SMALL, lambda j: jnp.where(j < n_kv_a, j + kv_first, kv_b),
                   n_kv_a + 1, 512, 1, F32).reshape(batch, a_keep, -1)
        mkv = _proj(memb, w_mem_b, l, TN_SMALL, lambda j: j, n_small_mem, batch * N_MEM, 1, F32)
        mk, mv = mkv[:, :M_WIDTH], mkv[:, M_WIDTH:]
        att = _attn_prompt(p, mk.astype(BF16), mv.astype(BF16), bias_a, bias_b, sink, batch, seq)
        h = _merge(xpb, att, w_gate, w_br, l, 1024, 1024, 4)
        xp, xpb = _outproj(xp, h, w_out_b, lng, lnb, l, tm_p, 4)
        xp, xpb = _ffn(xp, xpb, f2gu, f2d, lng, lnb, l, 2, *ffn_tiles_p)

        akp.append(kv[:, :, :A_WIDTH].reshape(batch, a_keep, A_HEADS, HEAD_DIM))
        avp.append(kv[:, :, A_WIDTH:2 * A_WIDTH].reshape(batch, a_keep, A_HEADS, HEAD_DIM))
        kvb = kv[:, a_keep - b_keep:, 2 * A_WIDTH:]
        bkp.append(kvb[:, :, :B_KV_WIDTH].reshape(batch, b_keep, B_KV_HEADS, HEAD_DIM))
        bvp.append(kvb[:, :, B_KV_WIDTH:].reshape(batch, b_keep, B_KV_HEADS, HEAD_DIM))
        mkp.append(mk.reshape(batch, N_MEM, M_HEADS, M_HEAD_DIM))
        mvp.append(mv.reshape(batch, N_MEM, M_HEADS, M_HEAD_DIM))

        xs, xsb = _ffn(xs, xsb, f1gu, f1d, lng, lnb, l, 0, *ffn_tiles_s)
        ps = _proj(xsb, w_qkv, l, TN_SMALL, lambda j: j, n_small_qkv, tm_s, 1, F32)
        att_s = _attn_sample(ps, cak, cav, cbk, cbv, cmk, cmv, s_biases, sink, l, streams, t)
        hs = _merge(xsb, att_s, w_gate, w_br, l, tm_s, D_MODEL, 1)
        xs, xsb = _outproj(xs, hs, w_out_b, lng, lnb, l, tm_s, 1)
        xs, xsb = _ffn(xs, xsb, f2gu, f2d, lng, lnb, l, 2, *ffn_tiles_s)

        ps3 = ps.reshape(streams, t, QKV_WIDTH)
        aks.append(ps3[:, :, COL_KA:COL_KA + A_WIDTH].reshape(streams, t, A_HEADS, HEAD_DIM))
        avs.append(ps3[:, :, COL_VA:COL_VA + A_WIDTH].reshape(streams, t, A_HEADS, HEAD_DIM))
        bks.append(ps3[:, :, COL_KB:COL_KB + B_KV_WIDTH].reshape(streams, t, B_KV_HEADS, HEAD_DIM))
        bvs.append(ps3[:, :, COL_VB:COL_VB + B_KV_WIDTH].reshape(streams, t, B_KV_HEADS, HEAD_DIM))

    return (xp.reshape(batch, seq, D_MODEL), xs.reshape(streams, t, D_MODEL),
            jnp.stack(akp), jnp.stack(avp), jnp.stack(bkp), jnp.stack(bvp), jnp.stack(mkp), jnp.stack(mvp),
            jnp.stack(aks), jnp.stack(avs), jnp.stack(bks), jnp.stack(bvs))
```
